```python
import math
import jax, jax.numpy as jnp
from jax import lax
import numpy as np

D_MODEL = 4096
BATCH = 4
SEQ = 4096
DEPTH = 4

N_MEM = 256
REC_WIDTH = 3 * D_MODEL // 4
XA_HEADS = 4
XA_HEAD_DIM = D_MODEL // (4 * XA_HEADS)
XA_WIDTH = XA_HEADS * XA_HEAD_DIM
IN_WIDTH = 2 * REC_WIDTH + 2 * XA_WIDTH
MIX_WIDTH = REC_WIDTH + XA_WIDTH
S5_GROUP = 16
S5_GROUPS = REC_WIDTH // S5_GROUP
S5_STATE = 64
S5_DT_MIN = 1e-3
S5_DT_MAX = 1e-1
LRU_BLOCK = 256
LRU_BLOCKS = REC_WIDTH // LRU_BLOCK
LRU_C = 8.0
CONV_W = 4
N_MIXERS = 2
N_A = (DEPTH + 1) // 2
N_B = DEPTH // 2
EPS = 1e-6

kernel_name = 'hybrid_s5_rglru_memxattn_sandwich'


def rmsnorm(x, g):
    xf = x.astype(jnp.float32)
    y = xf * lax.rsqrt(jnp.mean(xf * xf, axis=-1, keepdims=True) + EPS) * g.astype(jnp.float32)
    return y.astype(x.dtype)


def s5_mixer(u, lam_re, lam_im, log_step, b_re, b_im, c_re, c_im, d_skip, w_glu, b_glu):
    bsz, L, _ = u.shape
    f32 = jnp.float32
    uf = u.astype(f32).reshape(bsz, L, S5_GROUPS, S5_GROUP)
    lr = jnp.minimum(lam_re.astype(f32), -1e-4)
    li = lam_im.astype(f32)
    dt = jnp.exp(log_step.astype(f32))[:, None]
    mag = jnp.exp(lr * dt)
    ab_re = mag * jnp.cos(li * dt)
    ab_im = mag * jnp.sin(li * dt)
    den = lr * lr + li * li
    nr = ab_re - 1.0
    ni = ab_im
    coef_re = (nr * lr + ni * li) / den
    coef_im = (ni * lr - nr * li) / den
    br = b_re.astype(f32)
    bi = b_im.astype(f32)
    bb_re = coef_re[..., None] * br - coef_im[..., None] * bi
    bb_im = coef_re[..., None] * bi + coef_im[..., None] * br
    x_re = jnp.einsum('blgc,gpc->lbgp', uf, bb_re)
    x_im = jnp.einsum('blgc,gpc->lbgp', uf, bb_im)
    a_re = jnp.broadcast_to(ab_re, (L,) + ab_re.shape)
    a_im = jnp.broadcast_to(ab_im, (L,) + ab_im.shape)

    def combine(e1, e2):
        a1r, a1i, b1r, b1i = e1
        a2r, a2i, b2r, b2i = e2
        ar = a1r * a2r - a1i * a2i
        ai = a1r * a2i + a1i * a2r
        a2r_b = a2r[:, None]
        a2i_b = a2i[:, None]
        new_br = a2r_b * b1r - a2i_b * b1i + b2r
        new_bi = a2r_b * b1i + a2i_b * b1r + b2i
        return ar, ai, new_br, new_bi

    _, _, s_re, s_im = lax.associative_scan(combine, (a_re, a_im, x_re, x_im), axis=0)
    y = (jnp.einsum('lbgp,gcp->blgc', s_re, c_re.astype(f32))
         - jnp.einsum('lbgp,gcp->blgc', s_im, c_im.astype(f32)))
    y = y.reshape(bsz, L, REC_WIDTH) + d_skip.astype(f32) * uf.reshape(bsz, L, REC_WIDTH)
    g = jax.nn.gelu(y)
    out = g * jax.nn.sigmoid(g @ w_glu.astype(f32) + b_glu.astype(f32))
    return out.astype(u.dtype)


def rglru_mixer(u, conv_w, conv_b, w_a, b_a, w_x, b_x, lam):
    bsz, L, E = u.shape
    f32 = jnp.float32
    xc = lax.conv_general_dilated(
        u, conv_w.reshape(CONV_W, 1, E).astype(u.dtype),
        window_strides=(1,), padding=[(CONV_W - 1, 0)],
        dimension_numbers=('NWC', 'WIO', 'NWC'), feature_group_count=E) + conv_b
    xb = xc.reshape(bsz, L, LRU_BLOCKS, LRU_BLOCK)
    r = jax.nn.sigmoid((jnp.einsum('blhi,hij->blhj', xb, w_a) + b_a).astype(f32)).reshape(bsz, L, E)
    ig = jax.nn.sigmoid((jnp.einsum('blhi,hij->blhj', xb, w_x) + b_x).astype(f32)).reshape(bsz, L, E)
    log_a = -LRU_C * r * jax.nn.softplus(-lam.astype(f32))
    a = jnp.exp(log_a)
    mult = jnp.sqrt(-jnp.expm1(2.0 * log_a))
    mult = jnp.where((jnp.arange(L) == 0)[None, :, None], 1.0, mult)
    bt = mult * (ig * xc.astype(f32))

    def step(h, ab):
        a_t, b_t = ab
        h = a_t * h + b_t
        return h, h

    h0 = jnp.zeros((bsz, E), f32)
    _, hs = lax.scan(step, h0, (jnp.swapaxes(a, 0, 1), jnp.swapaxes(bt, 0, 1)))
    return jnp.swapaxes(hs, 0, 1).astype(u.dtype)


def memory_attention(q, mem_n, w_kv):
    bsz, L, _ = q.shape
    kv = mem_n @ w_kv
    k, v = jnp.split(kv, 2, axis=-1)
    qh = q.reshape(bsz, L, XA_HEADS, XA_HEAD_DIM)
    kh = k.reshape(bsz, N_MEM, XA_HEADS, XA_HEAD_DIM)
    vh = v.reshape(bsz, N_MEM, XA_HEADS, XA_HEAD_DIM)
    s = jnp.einsum('blhd,bnhd->bhln', qh, kh).astype(jnp.float32) * (XA_HEAD_DIM ** -0.5)
    p = jax.nn.softmax(s, axis=-1).astype(vh.dtype)
    o = jnp.einsum('bhln,bnhd->blhd', p, vh)
    return o.reshape(bsz, L, XA_WIDTH)


def setup_inputs(seed: int = 0) -> dict:
    key = jax.random.key(seed)
    ks = iter(jax.random.split(key, 40))
    f32 = jnp.float32

    def nrm(shape, scale):
        return scale * jax.random.normal(next(ks), shape, f32)

    x = nrm((BATCH, SEQ, D_MODEL), 1.0)
    mem = nrm((BATCH, N_MEM, D_MODEL), 1.0)
    w_in = nrm((DEPTH, D_MODEL, IN_WIDTH), D_MODEL ** -0.5)
    w_kv = nrm((DEPTH, D_MODEL, 2 * XA_WIDTH), D_MODEL ** -0.5)
    w_out = nrm((DEPTH, MIX_WIDTH, D_MODEL), MIX_WIDTH ** -0.5)
    pre_norm = 1.0 + nrm((DEPTH, D_MODEL), 0.02)
    post_norm = 1.0 + nrm((DEPTH, D_MODEL), 0.02)
    mem_norm = 1.0 + nrm((DEPTH, D_MODEL), 0.02)
    n_idx = jnp.arange(S5_STATE, dtype=f32)
    s5_lam_re = -0.5 + nrm((N_A, S5_GROUPS, S5_STATE), 0.01)
    s5_lam_im = math.pi * n_idx + nrm((N_A, S5_GROUPS, S5_STATE), 0.01)
    s5_log_step = jax.random.uniform(next(ks), (N_A, S5_GROUPS), f32,
                                     minval=math.log(S5_DT_MIN), maxval=math.log(S5_DT_MAX))
    s5_b_re = nrm((N_A, S5_GROUPS, S5_STATE, S5_GROUP), (2 * S5_GROUP) ** -0.5)
    s5_b_im = nrm((N_A, S5_GROUPS, S5_STATE, S5_GROUP), (2 * S5_GROUP) ** -0.5)
    s5_c_re = nrm((N_A, S5_GROUPS, S5_GROUP, S5_STATE), (2 * S5_STATE) ** -0.5)
    s5_c_im = nrm((N_A, S5_GROUPS, S5_GROUP, S5_STATE), (2 * S5_STATE) ** -0.5)
    s5_d = nrm((N_A, REC_WIDTH), 1.0)
    s5_w_glu = nrm((N_A, REC_WIDTH, REC_WIDTH), REC_WIDTH ** -0.5)
    s5_b_glu = nrm((N_A, REC_WIDTH), 0.01)
    lru_conv_w = nrm((N_B, CONV_W, REC_WIDTH), CONV_W ** -0.5)
    lru_conv_b = nrm((N_B, REC_WIDTH), 0.01)
    lru_w_a = nrm((N_B, LRU_BLOCKS, LRU_BLOCK, LRU_BLOCK), LRU_BLOCK ** -0.5)
    lru_b_a = nrm((N_B, LRU_BLOCKS, LRU_BLOCK), 0.01)
    lru_w_x = nrm((N_B, LRU_BLOCKS, LRU_BLOCK, LRU_BLOCK), LRU_BLOCK ** -0.5)
    lru_b_x = nrm((N_B, LRU_BLOCKS, LRU_BLOCK), 0.01)
    a_pow = jax.random.uniform(next(ks), (N_B, REC_WIDTH), f32, minval=0.9, maxval=0.999)
    a0 = a_pow ** (1.0 / LRU_C)
    lru_lam = jnp.log(a0) - jnp.log1p(-a0)
    return {'x': x, 'mem': mem, 'w_in': w_in, 'w_kv': w_kv, 'w_out': w_out,
            'pre_norm': pre_norm, 'post_norm': post_norm, 'mem_norm': mem_norm,
            's5_lam_re': s5_lam_re, 's5_lam_im': s5_lam_im, 's5_log_step': s5_log_step,
            's5_b_re': s5_b_re, 's5_b_im': s5_b_im, 's5_c_re': s5_c_re, 's5_c_im': s5_c_im,
            's5_d': s5_d, 's5_w_glu': s5_w_glu, 's5_b_glu': s5_b_glu,
            'lru_conv_w': lru_conv_w, 'lru_conv_b': lru_conv_b, 'lru_w_a': lru_w_a, 'lru_b_a': lru_b_a,
            'lru_w_x': lru_w_x, 'lru_b_x': lru_b_x, 'lru_lam': lru_lam}


def reference(x, mem, w_in, w_kv, w_out, pre_norm, post_norm, mem_norm,
              s5_lam_re, s5_lam_im, s5_log_step, s5_b_re, s5_b_im, s5_c_re, s5_c_im,
              s5_d, s5_w_glu, s5_b_glu,
              lru_conv_w, lru_conv_b, lru_w_a, lru_b_a, lru_w_x, lru_b_x, lru_lam):
    h = x
    for i in range(DEPTH):
        hn = rmsnorm(h, pre_norm[i])
        proj = hn @ w_in[i]
        u, gate, q, q_gate = jnp.split(
            proj, [REC_WIDTH, 2 * REC_WIDTH, 2 * REC_WIDTH + XA_WIDTH], axis=-1)
        if i % N_MIXERS == 0:
            j = i // N_MIXERS
            y = s5_mixer(u, s5_lam_re[j], s5_lam_im[j], s5_log_step[j], s5_b_re[j], s5_b_im[j],
                         s5_c_re[j], s5_c_im[j], s5_d[j], s5_w_glu[j], s5_b_glu[j])
        else:
            j = i // N_MIXERS
            y = rglru_mixer(u, lru_conv_w[j], lru_conv_b[j], lru_w_a[j], lru_b_a[j],
                            lru_w_x[j], lru_b_x[j], lru_lam[j])
        m = memory_attention(q, rmsnorm(mem, mem_norm[i]), w_kv[i])
        mixed = jnp.concatenate([y * jax.nn.silu(gate), m * jax.nn.silu(q_gate)], axis=-1)
        out = mixed @ w_out[i]
        h = h + rmsnorm(out, post_norm[i])
    return h
```

```python
import functools
import math

import jax
import jax.numpy as jnp
from jax import lax
from jax.experimental import pallas as pl
from jax.experimental.pallas import tpu as pltpu

F32 = jnp.float32
BF16 = jnp.bfloat16

EPS = 1e-6
XA_HEADS = 4
LRU_C = 8.0
TL = 16
CL = 16
VMEM_LIMIT_V7X = 56 * 1024 * 1024


def _tile(n, pref):
    if n <= pref:
        return n
    t = (pref // 128) * 128
    while n % t:
        t -= 128
    return t


def _params(semantics):
    return pltpu.CompilerParams(dimension_semantics=semantics, vmem_limit_bytes=VMEM_LIMIT_V7X)


def _silu(x):
    return x * jax.nn.sigmoid(x)


def _gelu_tanh(x):
    c = math.sqrt(2.0 / math.pi)
    return 0.5 * x * (1.0 + jnp.tanh(c * (x + 0.044715 * (x * x * x))))


def _norm_kernel(x_ref, g_ref, o_ref):
    x = x_ref[...]
    ms = jnp.mean(x * x, axis=-1, keepdims=True)
    o_ref[...] = (x * lax.rsqrt(ms + EPS) * g_ref[...]).astype(o_ref.dtype)


def _rmsnorm_cast(x, g):
    t, d = x.shape
    tm = _tile(t, 512)
    return pl.pallas_call(
        _norm_kernel,
        out_shape=jax.ShapeDtypeStruct((t, d), BF16),
        grid=(t // tm,),
        in_specs=[pl.BlockSpec((tm, d), lambda i: (i, 0)),
                  pl.BlockSpec((1, d), lambda i: (0, 0))],
        out_specs=pl.BlockSpec((tm, d), lambda i: (i, 0)),
        compiler_params=_params(("parallel",)),
        name="entry_norm",
    )(x, g.reshape(1, d))


def _matmul_kernel(x_ref, w_ref, o_ref):
    o_ref[...] = jnp.dot(x_ref[...], w_ref[...], preferred_element_type=F32).astype(o_ref.dtype)


def _in_proj(hn, w):
    t, d = hn.shape
    n = w.shape[1]
    tm, tn = _tile(t, 1024), _tile(n, 1024)
    return pl.pallas_call(
        _matmul_kernel,
        out_shape=jax.ShapeDtypeStruct((t, n), BF16),
        grid=(t // tm, n // tn),
        in_specs=[pl.BlockSpec((tm, d), lambda i, j: (i, 0)),
                  pl.BlockSpec((d, tn), lambda i, j: (0, j))],
        out_specs=pl.BlockSpec((tm, tn), lambda i, j: (i, j)),
        compiler_params=_params(("parallel", "arbitrary")),
        name="in_proj",
    )(hn, w)


def _kv_kernel(mem_ref, g_ref, w_ref, o_ref):
    x = mem_ref[...]
    ms = jnp.mean(x * x, axis=-1, keepdims=True)
    xn = (x * lax.rsqrt(ms + EPS) * g_ref[...]).astype(BF16)
    o_ref[...] = jnp.dot(xn, w_ref[...], preferred_element_type=F32).astype(o_ref.dtype)


def _kv_proj(mem, mem_norm, w_kv):
    b, n, d = mem.shape
    depth, _, n2 = w_kv.shape
    tn = _tile(n2, 512)
    return pl.pallas_call(
        _kv_kernel,
        out_shape=jax.ShapeDtypeStruct((depth, b, n, n2), BF16),
        grid=(depth, n2 // tn, b),
        in_specs=[pl.BlockSpec((None, n, d), lambda l, j, i: (i, 0, 0)),
                  pl.BlockSpec((None, 1, d), lambda l, j, i: (l, 0, 0)),
                  pl.BlockSpec((None, d, tn), lambda l, j, i: (l, 0, j))],
        out_specs=pl.BlockSpec((None, None, n, tn), lambda l, j, i: (l, i, 0, j)),
        compiler_params=_params(("parallel", "parallel", "arbitrary")),
        name="kv_proj",
    )(mem, mem_norm.reshape(depth, 1, d), w_kv)


def _attn_kernel(q_ref, qg_ref, kv_ref, o_ref, *, xa, hd):
    scale = hd ** -0.5
    for h in range(XA_HEADS):
        q = q_ref[:, h * hd:(h + 1) * hd]
        k = kv_ref[:, h * hd:(h + 1) * hd]
        v = kv_ref[:, xa + h * hd:xa + (h + 1) * hd]
        s = lax.dot_general(q, k, (((1,), (1,)), ((), ())), preferred_element_type=F32) * scale
        p = jnp.exp(s - jnp.max(s, axis=-1, keepdims=True))
        denom = jnp.sum(p, axis=-1, keepdims=True)
        o = jnp.dot(p.astype(BF16), v, preferred_element_type=F32) / denom
        qg = qg_ref[:, h * hd:(h + 1) * hd].astype(F32)
        o_ref[:, h * hd:(h + 1) * hd] = (o * _silu(qg)).astype(o_ref.dtype)


def _mem_attention(proj, kv_l, layer, batch, rec, xa):
    t = proj.shape[0]
    rows = t // batch
    n_mem = kv_l.shape[2]
    tm = _tile(rows, 1024)
    per_b = rows // tm
    qcol = (2 * rec) // xa
    hd = xa // XA_HEADS
    return pl.pallas_call(
        functools.partial(_attn_kernel, xa=xa, hd=hd),
        out_shape=jax.ShapeDtypeStruct((t, xa), BF16),
        grid=(batch, per_b),
        in_specs=[pl.BlockSpec((tm, xa), lambda b, i: (b * per_b + i, qcol)),
                  pl.BlockSpec((tm, xa), lambda b, i: (b * per_b + i, qcol + 1)),
                  pl.BlockSpec((None, None, n_mem, 2 * xa), lambda b, i: (layer, b, 0, 0))],
        out_specs=pl.BlockSpec((tm, xa), lambda b, i: (b * per_b + i, 0)),
        compiler_params=_params(("parallel", "arbitrary")),
        name="mem_attn",
    )(proj, proj, kv_l)


def _out_proj_kernel(y_ref, m_ref, w_ref, pg_ref, h_ref, ng_ref, ho_ref, hn_ref, *, ky, nk):
    k = pl.program_id(1)

    @pl.when(k == 0)
    def _():
        ho_ref[...] = jnp.zeros_like(ho_ref)

    @pl.when(k < ky)
    def _():
        ho_ref[...] += jnp.dot(y_ref[...], w_ref[...], preferred_element_type=F32)

    @pl.when(k >= ky)
    def _():
        ho_ref[...] += jnp.dot(m_ref[...], w_ref[...], preferred_element_type=F32)

    @pl.when(k == nk - 1)
    def _():
        o = ho_ref[...]
        ms = jnp.mean(o * o, axis=-1, keepdims=True)
        h = h_ref[...] + o * lax.rsqrt(ms + EPS) * pg_ref[...]
        ho_ref[...] = h
        ms2 = jnp.mean(h * h, axis=-1, keepdims=True)
        hn_ref[...] = (h * lax.rsqrt(ms2 + EPS) * ng_ref[...]).astype(hn_ref.dtype)


def _out_proj(ymix, mmix, w_out, post_g, h, next_g):
    t, rec = ymix.shape
    xa = mmix.shape[1]
    d = w_out.shape[1]
    tm = _tile(t, 256)
    tk = math.gcd(_tile(rec, 1024), xa)
    ky, km = rec // tk, xa // tk
    nk = ky + km
    return pl.pallas_call(
        functools.partial(_out_proj_kernel, ky=ky, nk=nk),
        out_shape=(jax.ShapeDtypeStruct((t, d), F32), jax.ShapeDtypeStruct((t, d), BF16)),
        grid=(t // tm, nk),
        in_specs=[pl.BlockSpec((tm, tk), lambda i, k: (i, jnp.minimum(k, ky - 1))),
                  pl.BlockSpec((tm, tk), lambda i, k: (i, jnp.maximum(k - ky, 0))),
                  pl.BlockSpec((tk, d), lambda i, k: (k, 0)),
                  pl.BlockSpec((1, d), lambda i, k: (0, 0)),
                  pl.BlockSpec((tm, d), lambda i, k: (i, 0)),
                  pl.BlockSpec((1, d), lambda i, k: (0, 0))],
        out_specs=(pl.BlockSpec((tm, d), lambda i, k: (i, 0)),
                   pl.BlockSpec((tm, d), lambda i, k: (i, 0))),
        compiler_params=_params(("parallel", "arbitrary")),
        name="out_proj",
    )(ymix, mmix, w_out, post_g.reshape(1, d), h, next_g.reshape(1, d))


def _cmul(s, a1, a2, half):
    return s * a1 + pltpu.roll(s, half, axis=1) * a2


def _shift_rows(x, k, period, fill):
    rows = lax.broadcasted_iota(jnp.int32, x.shape, 0)
    return jnp.where((rows & (period - 1)) < k, fill, pltpu.roll(x, k, axis=0))


def _s5_kernel(u_ref, m_ref, be_ref, cp_ref, lt_ref, d_ref, o_ref, l_ref, s_ref, *, gb, rs, ch, p):
    for g in range(gb):
        u = u_ref[g]
        z = jnp.dot(u, be_ref[g], preferred_element_type=F32)
        lt = lt_ref[g]
        a1, a2 = lt[0:1], lt[1:2]
        lc = z[0:rs]
        l_ref[0:rs, :] = lc
        for cl in range(1, CL):
            lc = _cmul(lc, a1, a2, p) + z[cl * rs:(cl + 1) * rs]
            l_ref[cl * rs:(cl + 1) * rs, :] = lc
        tot = lc
        k, row = 1, 2 + 2 * CL
        while k < ch:
            tot = tot + _cmul(_shift_rows(tot, k, ch, 0.0), lt[row:row + 1], lt[row + 1:row + 2], p)
            k, row = 2 * k, row + 2
        cy = _shift_rows(tot, 1, ch, 0.0)
        s_ref[0:rs, :] = cy.astype(s_ref.dtype)
        for cl in range(1, CL):
            b1, b2 = lt[2 + 2 * cl:3 + 2 * cl], lt[3 + 2 * cl:4 + 2 * cl]
            s_in = l_ref[(cl - 1) * rs:cl * rs, :] + _cmul(cy, b1, b2, p)
            s_ref[cl * rs:(cl + 1) * rs, :] = s_in.astype(s_ref.dtype)
        y = jnp.dot(u, m_ref[g], preferred_element_type=F32)
        y = y + jnp.dot(s_ref[...], cp_ref[g], preferred_element_type=F32)
        y = y + d_ref[g] * u.astype(F32)
        o_ref[g] = _gelu_tanh(y).astype(o_ref.dtype)


def _s5_tables(lam_re, lam_im, log_step, b_re, b_im, c_re, c_im, d_skip, ch):
    g, p, c = b_re.shape
    hi = lax.Precision.HIGHEST
    lr = jnp.minimum(lam_re.astype(F32), -1e-4)
    li = lam_im.astype(F32)
    dt = jnp.exp(log_step.astype(F32))[:, None]

    def lam_pow(k):
        k = jnp.asarray(k, F32)[..., None, None]
        mag = jnp.exp(lr * dt * k)
        return mag * jnp.cos(li * dt * k), mag * jnp.sin(li * dt * k)

    ab_re, ab_im = lam_pow(1.0)
    den = lr * lr + li * li
    nr, ni = ab_re - 1.0, ab_im
    coef_re = (nr * lr + ni * li) / den
    coef_im = (ni * lr - nr * li) / den
    br, bi = b_re.astype(F32), b_im.astype(F32)
    bb_re = coef_re[..., None] * br - coef_im[..., None] * bi
    bb_im = coef_re[..., None] * bi + coef_im[..., None] * br
    cr, ci = c_re.astype(F32), c_im.astype(F32)

    steps = jnp.arange(TL + 1)
    pr, pi = lam_pow(steps)
    cl_re = cr[None] * pr[:, :, None, :] - ci[None] * pi[:, :, None, :]
    cl_im = cr[None] * pi[:, :, None, :] + ci[None] * pr[:, :, None, :]
    klag = (jnp.einsum('kgop,gpi->gkoi', cl_re[:TL], bb_re, precision=hi)
            - jnp.einsum('kgop,gpi->gkoi', cl_im[:TL], bb_im, precision=hi))
    t_in = jnp.arange(TL)[:, None]
    t_out = jnp.arange(TL)[None, :]
    lag = t_out - t_in
    m = jnp.where((lag >= 0)[None, :, :, None, None], klag[:, jnp.maximum(lag, 0)], 0.0)
    m = m.transpose(0, 1, 4, 2, 3).reshape(g, TL * c, TL * c)

    qr, qi = pr[TL - 1 - jnp.arange(TL)], pi[TL - 1 - jnp.arange(TL)]
    be_re = qr[..., None] * bb_re[None] - qi[..., None] * bb_im[None]
    be_im = qr[..., None] * bb_im[None] + qi[..., None] * bb_re[None]
    bend = jnp.concatenate([be_re, be_im], axis=2)
    bend = bend.transpose(1, 0, 3, 2).reshape(g, TL * c, 2 * p)

    cp = jnp.concatenate([cl_re[1:], -cl_im[1:]], axis=3)
    cpow = cp.transpose(1, 3, 0, 2).reshape(g, 2 * p, TL * c)

    ks = [float(TL)] + [float(TL * cl) for cl in range(CL)]
    k = 1
    while k < ch:
        ks.append(float(TL * CL * k))
        k *= 2
    wr, wi = lam_pow(jnp.array(ks, F32))
    rows = jnp.stack([jnp.concatenate([wr, wr], axis=-1),
                      jnp.concatenate([-wi, wi], axis=-1)], axis=1)
    ltab = rows.reshape(2 * len(ks), g, 2 * p).transpose(1, 0, 2)
    pad = (-ltab.shape[1]) % 8
    ltab = jnp.pad(ltab, ((0, 0), (0, pad), (0, 0)))
    dtile = jnp.tile(d_skip.astype(F32).reshape(g, 1, c), (1, 1, TL))
    return m.astype(BF16), bend.astype(BF16), cpow.astype(BF16), ltab, dtile


def _s5_ssm(ug, tables, batch, ch):
    m, bend, cpow, ltab, dtile = tables
    g, r, w = ug.shape
    p2 = bend.shape[2]
    rs = batch * ch
    gb = 8
    while g % gb:
        gb //= 2
    nt = ltab.shape[1]
    return pl.pallas_call(
        functools.partial(_s5_kernel, gb=gb, rs=rs, ch=ch, p=p2 // 2),
        out_shape=jax.ShapeDtypeStruct((g, r, w), BF16),
        grid=(g // gb,),
        in_specs=[pl.BlockSpec((gb, r, w), lambda i: (i, 0, 0)),
                  pl.BlockSpec((gb, w, w), lambda i: (i, 0, 0)),
                  pl.BlockSpec((gb, w, p2), lambda i: (i, 0, 0)),
                  pl.BlockSpec((gb, p2, w), lambda i: (i, 0, 0)),
                  pl.BlockSpec((gb, nt, p2), lambda i: (i, 0, 0)),
                  pl.BlockSpec((gb, 1, w), lambda i: (i, 0, 0))],
        out_specs=pl.BlockSpec((gb, r, w), lambda i: (i, 0, 0)),
        scratch_shapes=[pltpu.VMEM((r, p2), F32), pltpu.VMEM((r, p2), BF16)],
        compiler_params=_params(("parallel",)),
        name="s5_ssm",
    )(ug, m, bend, cpow, ltab, dtile)


def _glu_kernel(g_ref, w_ref, b_ref, gn_ref, gate_ref, o_ref):
    z = jnp.dot(g_ref[...], w_ref[...], preferred_element_type=F32) + b_ref[...]
    gn = gn_ref[...].astype(F32)
    gate = gate_ref[...].astype(F32)
    o_ref[...] = (gn * jax.nn.sigmoid(z) * _silu(gate)).astype(o_ref.dtype)


def _glu_gate(gact, w_glu, b_glu, proj):
    t, rec = gact.shape
    tm, tn = _tile(t, 1024), _tile(rec, 1024)
    goff = rec // tn
    return pl.pallas_call(
        _glu_kernel,
        out_shape=jax.ShapeDtypeStruct((t, rec), BF16),
        grid=(t // tm, rec // tn),
        in_specs=[pl.BlockSpec((tm, rec), lambda i, j: (i, 0)),
                  pl.BlockSpec((rec, tn), lambda i, j: (0, j)),
                  pl.BlockSpec((1, tn), lambda i, j: (0, j)),
                  pl.BlockSpec((tm, tn), lambda i, j: (i, j)),
                  pl.BlockSpec((tm, tn), lambda i, j: (i, goff + j))],
        out_specs=pl.BlockSpec((tm, tn), lambda i, j: (i, j)),
        compiler_params=_params(("parallel", "arbitrary")),
        name="s5_glu",
    )(gact, w_glu, b_glu.reshape(1, rec), gact, proj)


def _lru_kernel(u_ref, gate_ref, cw_ref, cb_ref, wa_ref, ba_ref, wx_ref, bx_ref, lam_ref, o_ref,
                hs_ref, ps_ref, *, c):
    neg_lam = -lam_ref[...]
    sp = jnp.maximum(neg_lam, 0.0) + jnp.log1p(jnp.exp(-jnp.abs(neg_lam)))
    cw = cw_ref[...]
    cb = cb_ref[...]
    wa, wx = wa_ref[...], wx_ref[...]
    ba, bx = ba_ref[...], bx_ref[...]
    conv_w = cw.shape[0]

    def slab(tl):
        return u_ref[tl * c:(tl + 1) * c, :].astype(F32)

    h = None
    pr = None
    for tl in range(TL):
        xc = cb + cw[conv_w - 1:conv_w] * slab(tl)
        for j in range(1, conv_w):
            wj = cw[conv_w - 1 - j:conv_w - j]
            if tl - j >= 0:
                xc = xc + wj * slab(tl - j)
            else:
                xc = xc + wj * _shift_rows(slab(tl - j + TL), 1, c, 0.0)
        xcb = xc.astype(BF16)
        r = jax.nn.sigmoid(jnp.dot(xcb, wa, preferred_element_type=F32) + ba)
        ig = jax.nn.sigmoid(jnp.dot(xcb, wx, preferred_element_type=F32) + bx)
        log_a = (-LRU_C) * r * sp
        a = jnp.exp(log_a)
        mult = jnp.sqrt(1.0 - a * a)
        if tl == 0:
            rows = lax.broadcasted_iota(jnp.int32, mult.shape, 0)
            mult = jnp.where(rows == 0, 1.0, mult)
        bt = mult * (ig * xc)
        if tl == 0:
            h, pr = bt, a
        else:
            h, pr = a * h + bt, a * pr
        hs_ref[tl * c:(tl + 1) * c, :] = h
        ps_ref[tl * c:(tl + 1) * c, :] = pr
    av, bv = pr, h
    k = 1
    while k < c:
        bv = bv + av * _shift_rows(bv, k, c, 0.0)
        av = av * _shift_rows(av, k, c, 1.0)
        k *= 2
    carry = _shift_rows(bv, 1, c, 0.0)
    for tl in range(TL):
        hf = hs_ref[tl * c:(tl + 1) * c, :] + ps_ref[tl * c:(tl + 1) * c, :] * carry
        gate = gate_ref[tl * c:(tl + 1) * c, :].astype(F32)
        o_ref[tl * c:(tl + 1) * c, :] = (hf * _silu(gate)).astype(o_ref.dtype)


def _rglru(proj, conv_w, conv_b, w_a, b_a, w_x, b_x, lam, batch, rec):
    t = proj.shape[0]
    rows = t // batch
    nblk, blk, _ = w_a.shape
    cwid = conv_w.shape[0]
    goff = rec // blk
    vec = lambda v: v.astype(F32).reshape(nblk, 1, blk)
    return pl.pallas_call(
        functools.partial(_lru_kernel, c=rows // TL),
        out_shape=jax.ShapeDtypeStruct((t, rec), BF16),
        grid=(batch, nblk),
        in_specs=[pl.BlockSpec((rows, blk), lambda b, j: (b, j)),
                  pl.BlockSpec((rows, blk), lambda b, j: (b, goff + j)),
                  pl.BlockSpec((cwid, blk), lambda b, j: (0, j)),
                  pl.BlockSpec((None, 1, blk), lambda b, j: (j, 0, 0)),
                  pl.BlockSpec((None, blk, blk), lambda b, j: (j, 0, 0)),
                  pl.BlockSpec((None, 1, blk), lambda b, j: (j, 0, 0)),
                  pl.BlockSpec((None, blk, blk), lambda b, j: (j, 0, 0)),
                  pl.BlockSpec((None, 1, blk), lambda b, j: (j, 0, 0)),
                  pl.BlockSpec((None, 1, blk), lambda b, j: (j, 0, 0))],
        out_specs=pl.BlockSpec((rows, blk), lambda b, j: (b, j)),
        scratch_shapes=[pltpu.VMEM((rows, blk), F32), pltpu.VMEM((rows, blk), F32)],
        compiler_params=_params(("parallel", "arbitrary")),
        name="rglru",
    )(proj, proj, conv_w.astype(F32), vec(conv_b), w_a.astype(BF16), vec(b_a),
      w_x.astype(BF16), vec(b_x), vec(lam))


def kernel(x, mem, w_in, w_kv, w_out, pre_norm, post_norm, mem_norm, s5_lam_re, s5_lam_im, s5_log_step, s5_b_re, s5_b_im, s5_c_re, s5_c_im, s5_d, s5_w_glu, s5_b_glu, lru_conv_w, lru_conv_b, lru_w_a, lru_b_a, lru_w_x, lru_b_x, lru_lam):
    batch, seq, d = x.shape
    depth = w_in.shape[0]
    xa = w_kv.shape[2] // 2
    rec = w_out.shape[1] - xa
    groups, _, gch = s5_b_re.shape[1:]
    assert seq % (TL * CL) == 0 and TL * gch == 256 and rec == groups * gch
    ch = seq // (TL * CL)
    assert ch & (ch - 1) == 0, "row-shift masks assume power-of-two chunk counts"
    t = batch * seq

    h = x.reshape(batch, ch, CL, TL, d).transpose(0, 3, 1, 2, 4).reshape(t, d)
    kv = _kv_proj(mem, mem_norm, w_kv.astype(BF16))
    hn = _rmsnorm_cast(h, pre_norm[0])

    for i in range(depth):
        j = i // 2
        proj = _in_proj(hn, w_in[i].astype(BF16))
        if i % 2 == 0:
            tables = _s5_tables(s5_lam_re[j], s5_lam_im[j], s5_log_step[j], s5_b_re[j], s5_b_im[j],
                                s5_c_re[j], s5_c_im[j], s5_d[j], ch)
            ug = proj[:, :rec].reshape(batch, TL, ch, CL, groups, gch)
            ug = ug.transpose(4, 3, 0, 2, 1, 5).reshape(groups, CL * batch * ch, TL * gch)
            gact = _s5_ssm(ug, tables, batch, ch)
            gact = gact.reshape(groups, CL, batch, ch, TL, gch).transpose(2, 4, 3, 1, 0, 5).reshape(t, rec)
            ymix = _glu_gate(gact, s5_w_glu[j].astype(BF16), s5_b_glu[j].astype(F32), proj)
        else:
            ymix = _rglru(proj, lru_conv_w[j], lru_conv_b[j], lru_w_a[j], lru_b_a[j],
                          lru_w_x[j], lru_b_x[j], lru_lam[j], batch, rec)
        mmix = _mem_attention(proj, kv, i, batch, rec, xa)
        next_g = pre_norm[i + 1] if i + 1 < depth else pre_norm[i]
        h, hn = _out_proj(ymix, mmix, w_out[i].astype(BF16), post_norm[i], h, next_g)

    return h.reshape(batch, TL, ch, CL, d).transpose(0, 2, 3, 1, 4).reshape(batch, seq, d)
```

```python
import functools
import math

import jax
import jax.numpy as jnp
from jax import lax
from jax.experimental import pallas as pl
from jax.experimental.pallas import tpu as pltpu

F32 = jnp.float32
BF16 = jnp.bfloat16

EPS = 1e-6
XA_HEADS = 4
LRU_C = 8.0
TL = 16
LANE_SLAB = 256
VMEM_LIMIT_V7X = 56 * 1024 * 1024


def _tile(n, pref):
    if n <= pref:
        return n
    t = (pref // 128) * 128
    while n % t:
        t -= 128
    return t


def _params(semantics):
    return pltpu.CompilerParams(dimension_semantics=semantics, vmem_limit_bytes=VMEM_LIMIT_V7X)


def _silu(x):
    return x * jax.nn.sigmoid(x)


def _gelu_tanh(x):
    c = math.sqrt(2.0 / math.pi)
    return 0.5 * x * (1.0 + jnp.tanh(c * (x + 0.044715 * (x * x * x))))


def _norm_kernel(x_ref, g_ref, o_ref):
    x = x_ref[...]
    ms = jnp.mean(x * x, axis=-1, keepdims=True)
    o_ref[...] = (x * lax.rsqrt(ms + EPS) * g_ref[...]).astype(o_ref.dtype)


def _rmsnorm_cast(x, g):
    t, d = x.shape
    tm = _tile(t, 512)
    return pl.pallas_call(
        _norm_kernel,
        out_shape=jax.ShapeDtypeStruct((t, d), BF16),
        grid=(t // tm,),
        in_specs=[pl.BlockSpec((tm, d), lambda i: (i, 0)),
                  pl.BlockSpec((1, d), lambda i: (0, 0))],
        out_specs=pl.BlockSpec((tm, d), lambda i: (i, 0)),
        compiler_params=_params(("parallel",)),
        name="entry_norm",
    )(x, g.reshape(1, d))


def _matmul_kernel(x_ref, w_ref, o_ref):
    o_ref[...] = jnp.dot(x_ref[...], w_ref[...], preferred_element_type=F32).astype(o_ref.dtype)


def _in_proj(hn, w):
    t, d = hn.shape
    n = w.shape[1]
    tm, tn = _tile(t, 1024), _tile(n, 1024)
    return pl.pallas_call(
        _matmul_kernel,
        out_shape=jax.ShapeDtypeStruct((t, n), BF16),
        grid=(t // tm, n // tn),
        in_specs=[pl.BlockSpec((tm, d), lambda i, j: (i, 0)),
                  pl.BlockSpec((d, tn), lambda i, j: (0, j))],
        out_specs=pl.BlockSpec((tm, tn), lambda i, j: (i, j)),
        compiler_params=_params(("parallel", "arbitrary")),
        name="in_proj",
    )(hn, w)


def _kv_kernel(mem_ref, g_ref, w_ref, o_ref):
    x = mem_ref[...]
    ms = jnp.mean(x * x, axis=-1, keepdims=True)
    xn = (x * lax.rsqrt(ms + EPS) * g_ref[...]).astype(BF16)
    o_ref[...] = jnp.dot(xn, w_ref[...], preferred_element_type=F32).astype(o_ref.dtype)


def _kv_proj(mem, mem_norm, w_kv):
    b, n, d = mem.shape
    depth, _, n2 = w_kv.shape
    tn = _tile(n2, 512)
    return pl.pallas_call(
        _kv_kernel,
        out_shape=jax.ShapeDtypeStruct((depth, b, n, n2), BF16),
        grid=(depth, n2 // tn, b),
        in_specs=[pl.BlockSpec((None, n, d), lambda l, j, i: (i, 0, 0)),
                  pl.BlockSpec((None, 1, d), lambda l, j, i: (l, 0, 0)),
                  pl.BlockSpec((None, d, tn), lambda l, j, i: (l, 0, j))],
        out_specs=pl.BlockSpec((None, None, n, tn), lambda l, j, i: (l, i, 0, j)),
        compiler_params=_params(("parallel", "parallel", "arbitrary")),
        name="kv_proj",
    )(mem, mem_norm.reshape(depth, 1, d), w_kv)


def _attn_kernel(q_ref, qg_ref, kv_ref, o_ref, *, xa, hd):
    scale = hd ** -0.5
    for h in range(XA_HEADS):
        q = q_ref[:, h * hd:(h + 1) * hd]
        k = kv_ref[:, h * hd:(h + 1) * hd]
        v = kv_ref[:, xa + h * hd:xa + (h + 1) * hd]
        s = lax.dot_general(q, k, (((1,), (1,)), ((), ())), preferred_element_type=F32) * scale
        p = jnp.exp(s - jnp.max(s, axis=-1, keepdims=True))
        denom = jnp.sum(p, axis=-1, keepdims=True)
        o = jnp.dot(p.astype(BF16), v, preferred_element_type=F32) / denom
        qg = qg_ref[:, h * hd:(h + 1) * hd].astype(F32)
        o_ref[:, h * hd:(h + 1) * hd] = (o * _silu(qg)).astype(o_ref.dtype)


def _mem_attention(proj, kv_l, layer, batch, rec, xa):
    t = proj.shape[0]
    rows = t // batch
    n_mem = kv_l.shape[2]
    tm = _tile(rows, 1024)
    per_b = rows // tm
    qcol = (2 * rec) // xa
    hd = xa // XA_HEADS
    return pl.pallas_call(
        functools.partial(_attn_kernel, xa=xa, hd=hd),
        out_shape=jax.ShapeDtypeStruct((t, xa), BF16),
        grid=(batch, per_b),
        in_specs=[pl.BlockSpec((tm, xa), lambda b, i: (b * per_b + i, qcol)),
                  pl.BlockSpec((tm, xa), lambda b, i: (b * per_b + i, qcol + 1)),
                  pl.BlockSpec((None, None, n_mem, 2 * xa), lambda b, i: (layer, b, 0, 0))],
        out_specs=pl.BlockSpec((tm, xa), lambda b, i: (b * per_b + i, 0)),
        compiler_params=_params(("parallel", "arbitrary")),
        name="mem_attn",
    )(proj, proj, kv_l)


def _out_proj_kernel(y_ref, m_ref, w_ref, pg_ref, h_hbm, ng_ref, ho_ref, *rest, ky, nk, tm, emit_hn):
    if emit_hn:
        hn_ref, hbuf, sem = rest
    else:
        hbuf, sem = rest
    i = pl.program_id(0)
    k = pl.program_id(1)
    h_copy = pltpu.make_async_copy(h_hbm.at[pl.ds(i * tm, tm), :], hbuf, sem)

    @pl.when(k == 0)
    def _():
        h_copy.start()
        ho_ref[...] = jnp.zeros_like(ho_ref)

    @pl.when(k < ky)
    def _():
        ho_ref[...] += jnp.dot(y_ref[...], w_ref[...], preferred_element_type=F32)

    @pl.when(k >= ky)
    def _():
        ho_ref[...] += jnp.dot(m_ref[...], w_ref[...], preferred_element_type=F32)

    @pl.when(k == nk - 1)
    def _():
        h_copy.wait()
        o = ho_ref[...]
        ms = jnp.mean(o * o, axis=-1, keepdims=True)
        h = hbuf[...] + o * lax.rsqrt(ms + EPS) * pg_ref[...]
        ho_ref[...] = h
        if emit_hn:
            ms2 = jnp.mean(h * h, axis=-1, keepdims=True)
            hn_ref[...] = (h * lax.rsqrt(ms2 + EPS) * ng_ref[...]).astype(hn_ref.dtype)


def _out_proj(ymix, mmix, w_out, post_g, h, next_g):
    t, rec = ymix.shape
    xa = mmix.shape[1]
    d = w_out.shape[1]
    emit_hn = next_g is not None
    tm = _tile(t, 512)
    tk = math.gcd(_tile(rec, 512), xa)
    ky, km = rec // tk, xa // tk
    nk = ky + km
    row_spec = pl.BlockSpec((tm, d), lambda i, k: (i, 0))
    out_shape = [jax.ShapeDtypeStruct((t, d), F32)]
    out_specs = [row_spec]
    if emit_hn:
        out_shape.append(jax.ShapeDtypeStruct((t, d), BF16))
        out_specs.append(row_spec)
    else:
        next_g = post_g
    res = pl.pallas_call(
        functools.partial(_out_proj_kernel, ky=ky, nk=nk, tm=tm, emit_hn=emit_hn),
        out_shape=tuple(out_shape),
        grid=(t // tm, nk),
        in_specs=[pl.BlockSpec((tm, tk), lambda i, k: (i, jnp.minimum(k, ky - 1))),
                  pl.BlockSpec((tm, tk), lambda i, k: (i, jnp.maximum(k - ky, 0))),
                  pl.BlockSpec((tk, d), lambda i, k: (k, 0)),
                  pl.BlockSpec((1, d), lambda i, k: (0, 0)),
                  pl.BlockSpec(memory_space=pl.ANY),
                  pl.BlockSpec((1, d), lambda i, k: (0, 0))],
        out_specs=tuple(out_specs),
        scratch_shapes=[pltpu.VMEM((tm, d), F32), pltpu.SemaphoreType.DMA(())],
        compiler_params=_params(("arbitrary", "arbitrary")),
        name="out_proj",
    )(ymix, mmix, w_out, post_g.reshape(1, d), h, next_g.reshape(1, d))
    return (res[0], res[1]) if emit_hn else (res[0], None)


def _cmul(s, a1, a2, half):
    return s * a1 + pltpu.roll(s, half, axis=1) * a2


def _shift_rows(x, k, period, fill):
    rows = lax.broadcasted_iota(jnp.int32, x.shape, 0)
    return jnp.where((rows & (period - 1)) < k, fill, pltpu.roll(x, k, axis=0))


def _s5_kernel(u_ref, m_ref, be_ref, cp_ref, lt_ref, d_ref, o_ref, t1_ref, ug_ref, z_ref, s_ref, t2_ref,
               *, c, gs, gch, p):
    for tl in range(TL):
        t1_ref[tl] = u_ref[tl * c:(tl + 1) * c, :].astype(F32).T
    for g in range(gs):
        vt = t1_ref[:, g * gch:(g + 1) * gch, :].reshape(TL * gch, c)
        ug = vt.T.astype(BF16)
        ug_ref[g] = ug
        z_ref[g * c:(g + 1) * c, :] = jnp.dot(ug, be_ref[g], preferred_element_type=F32)
    z = z_ref[...]
    k, row = 1, 0
    while k < c:
        a1 = jnp.broadcast_to(lt_ref[:, row:row + 1, :], (gs, c, 2 * p)).reshape(gs * c, 2 * p)
        a2 = jnp.broadcast_to(lt_ref[:, row + 1:row + 2, :], (gs, c, 2 * p)).reshape(gs * c, 2 * p)
        z = z + _cmul(_shift_rows(z, k, c, 0.0), a1, a2, p)
        k, row = 2 * k, row + 2
    s_ref[...] = _shift_rows(z, 1, c, 0.0).astype(s_ref.dtype)
    for g in range(gs):
        ug = ug_ref[g]
        y = jnp.dot(ug, m_ref[g], preferred_element_type=F32)
        y = y + jnp.dot(s_ref[g * c:(g + 1) * c, :], cp_ref[g], preferred_element_type=F32)
        y = y + d_ref[g] * ug.astype(F32)
        t2_ref[:, g * gch:(g + 1) * gch, :] = _gelu_tanh(y).T.reshape(TL, gch, c)
    for tl in range(TL):
        o_ref[tl * c:(tl + 1) * c, :] = t2_ref[tl].T.astype(o_ref.dtype)


def _s5_tables(lam_re, lam_im, log_step, b_re, b_im, c_re, c_im, d_skip, n_chunks):
    g, p, c = b_re.shape
    hi = lax.Precision.HIGHEST
    lr = jnp.minimum(lam_re.astype(F32), -1e-4)
    li = lam_im.astype(F32)
    dt = jnp.exp(log_step.astype(F32))[:, None]

    def lam_pow(k):
        k = jnp.asarray(k, F32)[..., None, None]
        mag = jnp.exp(lr * dt * k)
        return mag * jnp.cos(li * dt * k), mag * jnp.sin(li * dt * k)

    ab_re, ab_im = lam_pow(1.0)
    den = lr * lr + li * li
    nr, ni = ab_re - 1.0, ab_im
    coef_re = (nr * lr + ni * li) / den
    coef_im = (ni * lr - nr * li) / den
    br, bi = b_re.astype(F32), b_im.astype(F32)
    bb_re = coef_re[..., None] * br - coef_im[..., None] * bi
    bb_im = coef_re[..., None] * bi + coef_im[..., None] * br
    cr, ci = c_re.astype(F32), c_im.astype(F32)

    steps = jnp.arange(TL + 1)
    pr, pi = lam_pow(steps)
    cl_re = cr[None] * pr[:, :, None, :] - ci[None] * pi[:, :, None, :]
    cl_im = cr[None] * pi[:, :, None, :] + ci[None] * pr[:, :, None, :]
    klag = (jnp.einsum('kgop,gpi->gkoi', cl_re[:TL], bb_re, precision=hi)
            - jnp.einsum('kgop,gpi->gkoi', cl_im[:TL], bb_im, precision=hi))
    t_in = jnp.arange(TL)[:, None]
    t_out = jnp.arange(TL)[None, :]
    lag = t_out - t_in
    m = jnp.where((lag >= 0)[None, :, :, None, None], klag[:, jnp.maximum(lag, 0)], 0.0)
    m = m.transpose(0, 1, 4, 2, 3).reshape(g, TL * c, TL * c)

    qr, qi = pr[TL - 1 - jnp.arange(TL)], pi[TL - 1 - jnp.arange(TL)]
    be_re = qr[..., None] * bb_re[None] - qi[..., None] * bb_im[None]
    be_im = qr[..., None] * bb_im[None] + qi[..., None] * bb_re[None]
    bend = jnp.concatenate([be_re, be_im], axis=2)
    bend = bend.transpose(1, 0, 3, 2).reshape(g, TL * c, 2 * p)

    cp = jnp.concatenate([cl_re[1:], -cl_im[1:]], axis=3)
    cpow = cp.transpose(1, 3, 0, 2).reshape(g, 2 * p, TL * c)

    ks = []
    k = 1
    while k < n_chunks:
        ks.append(float(TL * k))
        k *= 2
    wr, wi = lam_pow(jnp.array(ks, F32))
    rows = jnp.stack([jnp.concatenate([wr, wr], axis=-1),
                      jnp.concatenate([-wi, wi], axis=-1)], axis=1)
    ltab = rows.reshape(2 * len(ks), g, 2 * p).transpose(1, 0, 2)
    pad = (-ltab.shape[1]) % 8
    ltab = jnp.pad(ltab, ((0, 0), (0, pad), (0, 0)))
    dtile = jnp.tile(d_skip.astype(F32).reshape(g, 1, c), (1, 1, TL))
    return m.astype(BF16), bend.astype(BF16), cpow.astype(BF16), ltab, dtile


def _s5_ssm(proj, tables, batch, rec):
    m, bend, cpow, ltab, dtile = tables
    t = proj.shape[0]
    rows = t // batch
    c = rows // TL
    g, w, p2 = bend.shape
    gch = w // TL
    gs = LANE_SLAB // gch
    nt = ltab.shape[1]
    wspec = lambda *shape: pl.BlockSpec((gs,) + shape, lambda j, b: (j, 0, 0))
    return pl.pallas_call(
        functools.partial(_s5_kernel, c=c, gs=gs, gch=gch, p=p2 // 2),
        out_shape=jax.ShapeDtypeStruct((t, rec), BF16),
        grid=(g // gs, batch),
        in_specs=[pl.BlockSpec((rows, LANE_SLAB), lambda j, b: (b, j)),
                  wspec(w, w), wspec(w, p2), wspec(p2, w), wspec(nt, p2), wspec(1, w)],
        out_specs=pl.BlockSpec((rows, LANE_SLAB), lambda j, b: (b, j)),
        scratch_shapes=[pltpu.VMEM((TL, LANE_SLAB, c), F32),
                        pltpu.VMEM((gs, c, w), BF16),
                        pltpu.VMEM((gs * c, p2), F32),
                        pltpu.VMEM((gs * c, p2), BF16),
                        pltpu.VMEM((TL, LANE_SLAB, c), F32)],
        compiler_params=_params(("parallel", "arbitrary")),
        name="s5_ssm",
    )(proj, m, bend, cpow, ltab, dtile)


def _glu_kernel(g_ref, w_ref, b_ref, gn_ref, gate_ref, o_ref):
    z = jnp.dot(g_ref[...], w_ref[...], preferred_element_type=F32) + b_ref[...]
    gn = gn_ref[...].astype(F32)
    gate = gate_ref[...].astype(F32)
    o_ref[...] = (gn * jax.nn.sigmoid(z) * _silu(gate)).astype(o_ref.dtype)


def _glu_gate(gact, w_glu, b_glu, proj):
    t, rec = gact.shape
    tm, tn = _tile(t, 1024), _tile(rec, 1024)
    goff = rec // tn
    return pl.pallas_call(
        _glu_kernel,
        out_shape=jax.ShapeDtypeStruct((t, rec), BF16),
        grid=(t // tm, rec // tn),
        in_specs=[pl.BlockSpec((tm, rec), lambda i, j: (i, 0)),
                  pl.BlockSpec((rec, tn), lambda i, j: (0, j)),
                  pl.BlockSpec((1, tn), lambda i, j: (0, j)),
                  pl.BlockSpec((tm, tn), lambda i, j: (i, j)),
                  pl.BlockSpec((tm, tn), lambda i, j: (i, goff + j))],
        out_specs=pl.BlockSpec((tm, tn), lambda i, j: (i, j)),
        compiler_params=_params(("parallel", "arbitrary")),
        name="s5_glu",
    )(gact, w_glu, b_glu.reshape(1, rec), gact, proj)


def _lru_kernel(u_ref, gate_ref, cw_ref, cb_ref, wa_ref, ba_ref, wx_ref, bx_ref, lam_ref, o_ref,
                hs_ref, ps_ref, *, c):
    neg_lam = -lam_ref[...]
    sp = jnp.maximum(neg_lam, 0.0) + jnp.log1p(jnp.exp(-jnp.abs(neg_lam)))
    cw = cw_ref[...]
    cb = cb_ref[...]
    wa, wx = wa_ref[...], wx_ref[...]
    ba, bx = ba_ref[...], bx_ref[...]
    conv_w = cw.shape[0]

    def slab(tl):
        return u_ref[tl * c:(tl + 1) * c, :].astype(F32)

    h = None
    pr = None
    for tl in range(TL):
        xc = cb + cw[conv_w - 1:conv_w] * slab(tl)
        for j in range(1, conv_w):
            wj = cw[conv_w - 1 - j:conv_w - j]
            if tl - j >= 0:
                xc = xc + wj * slab(tl - j)
            else:
                xc = xc + wj * _shift_rows(slab(tl - j + TL), 1, c, 0.0)
        xcb = xc.astype(BF16)
        r = jax.nn.sigmoid(jnp.dot(xcb, wa, preferred_element_type=F32) + ba)
        ig = jax.nn.sigmoid(jnp.dot(xcb, wx, preferred_element_type=F32) + bx)
        log_a = (-LRU_C) * r * sp
        a = jnp.exp(log_a)
        mult = jnp.sqrt(1.0 - a * a)
        if tl == 0:
            rows = lax.broadcasted_iota(jnp.int32, mult.shape, 0)
            mult = jnp.where(rows == 0, 1.0, mult)
        bt = mult * (ig * xc)
        if tl == 0:
            h, pr = bt, a
        else:
            h, pr = a * h + bt, a * pr
        hs_ref[tl * c:(tl + 1) * c, :] = h
        ps_ref[tl * c:(tl + 1) * c, :] = pr
    av, bv = pr, h
    k = 1
    while k < c:
        bv = bv + av * _shift_rows(bv, k, c, 0.0)
        av = av * _shift_rows(av, k, c, 1.0)
        k *= 2
    carry = _shift_rows(bv, 1, c, 0.0)
    for tl in range(TL):
        hf = hs_ref[tl * c:(tl + 1) * c, :] + ps_ref[tl * c:(tl + 1) * c, :] * carry
        gate = gate_ref[tl * c:(tl + 1) * c, :].astype(F32)
        o_ref[tl * c:(tl + 1) * c, :] = (hf * _silu(gate)).astype(o_ref.dtype)


def _rglru(proj, conv_w, conv_b, w_a, b_a, w_x, b_x, lam, batch, rec):
    t = proj.shape[0]
    rows = t // batch
    nblk, blk, _ = w_a.shape
    cwid = conv_w.shape[0]
    goff = rec // blk
    vec = lambda v: v.astype(F32).reshape(nblk, 1, blk)
    return pl.pallas_call(
        functools.partial(_lru_kernel, c=rows // TL),
        out_shape=jax.ShapeDtypeStruct((t, rec), BF16),
        grid=(batch, nblk),
        in_specs=[pl.BlockSpec((rows, blk), lambda b, j: (b, j)),
                  pl.BlockSpec((rows, blk), lambda b, j: (b, goff + j)),
                  pl.BlockSpec((cwid, blk), lambda b, j: (0, j)),
                  pl.BlockSpec((None, 1, blk), lambda b, j: (j, 0, 0)),
                  pl.BlockSpec((None, blk, blk), lambda b, j: (j, 0, 0)),
                  pl.BlockSpec((None, 1, blk), lambda b, j: (j, 0, 0)),
                  pl.BlockSpec((None, blk, blk), lambda b, j: (j, 0, 0)),
                  pl.BlockSpec((None, 1, blk), lambda b, j: (j, 0, 0)),
                  pl.BlockSpec((None, 1, blk), lambda b, j: (j, 0, 0))],
        out_specs=pl.BlockSpec((rows, blk), lambda b, j: (b, j)),
        scratch_shapes=[pltpu.VMEM((rows, blk), F32), pltpu.VMEM((rows, blk), F32)],
        compiler_params=_params(("parallel", "arbitrary")),
        name="rglru",
    )(proj, proj, conv_w.astype(F32), vec(conv_b), w_a.astype(BF16), vec(b_a),
      w_x.astype(BF16), vec(b_x), vec(lam))


def kernel(x, mem, w_in, w_kv, w_out, pre_norm, post_norm, mem_norm, s5_lam_re, s5_lam_im, s5_log_step, s5_b_re, s5_b_im, s5_c_re, s5_c_im, s5_d, s5_w_glu, s5_b_glu, lru_conv_w, lru_conv_b, lru_w_a, lru_b_a, lru_w_x, lru_b_x, lru_lam):
    batch, seq, d = x.shape
    depth = w_in.shape[0]
    xa = w_kv.shape[2] // 2
    rec = w_out.shape[1] - xa
    groups, _, gch = s5_b_re.shape[1:]
    assert seq % TL == 0 and TL * gch == LANE_SLAB and rec == groups * gch
    c = seq // TL
    assert c & (c - 1) == 0, "row-shift masks assume a power-of-two chunk count"
    t = batch * seq

    h = x.reshape(batch, c, TL, d).transpose(0, 2, 1, 3).reshape(t, d)
    kv = _kv_proj(mem, mem_norm, w_kv.astype(BF16))
    hn = _rmsnorm_cast(h, pre_norm[0])

    for i in range(depth):
        j = i // 2
        proj = _in_proj(hn, w_in[i].astype(BF16))
        if i % 2 == 0:
            tables = _s5_tables(s5_lam_re[j], s5_lam_im[j], s5_log_step[j], s5_b_re[j], s5_b_im[j],
                                s5_c_re[j], s5_c_im[j], s5_d[j], c)
            gact = _s5_ssm(proj, tables, batch, rec)
            ymix = _glu_gate(gact, s5_w_glu[j].astype(BF16), s5_b_glu[j].astype(F32), proj)
        else:
            ymix = _rglru(proj, lru_conv_w[j], lru_conv_b[j], lru_w_a[j], lru_b_a[j],
                          lru_w_x[j], lru_b_x[j], lru_lam[j], batch, rec)
        mmix = _mem_attention(proj, kv, i, batch, rec, xa)
        next_g = pre_norm[i + 1] if i + 1 < depth else None
        h, hn = _out_proj(ymix, mmix, w_out[i].astype(BF16), post_norm[i], h, next_g)

    return h.reshape(batch, TL, c, d).transpose(0, 2, 1, 3).reshape(batch, seq, d)
```

```python
import functools
import math

import jax
import jax.numpy as jnp
from jax import lax
from jax.experimental import pallas as pl
from jax.experimental.pallas import tpu as pltpu

F32 = jnp.float32
BF16 = jnp.bfloat16

EPS = 1e-6
XA_HEADS = 4
LRU_C = 8.0
TL = 16
CL = 16
LANE_SLAB = 256
VMEM_LIMIT_V7X = 56 * 1024 * 1024


def _tile(n, pref):
    if n <= pref:
        return n
    t = (pref // 128) * 128
    while n % t:
        t -= 128
    return t


def _params(semantics):
    return pltpu.CompilerParams(dimension_semantics=semantics, vmem_limit_bytes=VMEM_LIMIT_V7X)


def _silu(x):
    return x * jax.nn.sigmoid(x)


def _gelu_tanh(x):
    c = math.sqrt(2.0 / math.pi)
    return 0.5 * x * (1.0 + jnp.tanh(c * (x + 0.044715 * (x * x * x))))


def _rms_scale(x):
    return lax.rsqrt(jnp.mean(x * x, axis=-1, keepdims=True) + EPS)


def _entry_kernel(x_ref, g_ref, h_ref, hn_ref):
    g = g_ref[...]
    for ch in range(x_ref.shape[0]):
        h_ref[:, ch, :] = x_ref[ch]
    for tl in range(TL):
        x = h_ref[tl]
        hn_ref[tl] = (x * _rms_scale(x) * g).astype(hn_ref.dtype)


def _entry(x, g):
    b, seq, d = x.shape
    ch = seq // (TL * CL)
    out_spec = pl.BlockSpec((None, TL, None, ch, d), lambda i, j: (i, 0, j, 0, 0))
    h, hn = pl.pallas_call(
        _entry_kernel,
        out_shape=(jax.ShapeDtypeStruct((b, TL, CL, ch, d), F32),
                   jax.ShapeDtypeStruct((b, TL, CL, ch, d), BF16)),
        grid=(b, CL),
        in_specs=[pl.BlockSpec((None, ch, None, TL, d), lambda i, j: (i, 0, j, 0, 0)),
                  pl.BlockSpec((1, d), lambda i, j: (0, 0))],
        out_specs=(out_spec, out_spec),
        compiler_params=_params(("parallel", "parallel")),
        name="entry_norm",
    )(x.reshape(b, ch, CL, TL, d), g.reshape(1, d))
    return h.reshape(b * seq, d), hn.reshape(b * seq, d)


def _matmul_kernel(x_ref, w_ref, o_ref):
    o_ref[...] = jnp.dot(x_ref[...], w_ref[...], preferred_element_type=F32).astype(o_ref.dtype)


def _in_proj(hn, w):
    t, d = hn.shape
    n = w.shape[1]
    tm, tn = _tile(t, 1024), _tile(n, 1024)
    return pl.pallas_call(
        _matmul_kernel,
        out_shape=jax.ShapeDtypeStruct((t, n), BF16),
        grid=(t // tm, n // tn),
        in_specs=[pl.BlockSpec((tm, d), lambda i, j: (i, 0)),
                  pl.BlockSpec((d, tn), lambda i, j: (0, j))],
        out_specs=pl.BlockSpec((tm, tn), lambda i, j: (i, j)),
        compiler_params=_params(("parallel", "arbitrary")),
        name="in_proj",
    )(hn, w)


def _out_matmul_kernel(y_ref, m_ref, w_ref, o_ref, *, rec):
    acc = jnp.dot(y_ref[...], w_ref[0:rec, :], preferred_element_type=F32)
    acc = acc + jnp.dot(m_ref[...], w_ref[rec:, :], preferred_element_type=F32)
    o_ref[...] = acc.astype(o_ref.dtype)


def _out_proj(ymix, mmix, w_out):
    t, rec = ymix.shape
    xa = mmix.shape[1]
    d = w_out.shape[1]
    tm, tn = _tile(t, 1024), _tile(d, 1024)
    return pl.pallas_call(
        functools.partial(_out_matmul_kernel, rec=rec),
        out_shape=jax.ShapeDtypeStruct((t, d), BF16),
        grid=(t // tm, d // tn),
        in_specs=[pl.BlockSpec((tm, rec), lambda i, j: (i, 0)),
                  pl.BlockSpec((tm, xa), lambda i, j: (i, 0)),
                  pl.BlockSpec((rec + xa, tn), lambda i, j: (0, j))],
        out_specs=pl.BlockSpec((tm, tn), lambda i, j: (i, j)),
        compiler_params=_params(("parallel", "arbitrary")),
        name="out_proj",
    )(ymix, mmix, w_out)


def _resid_kernel(o_ref, h_ref, pg_ref, ng_ref, ho_ref, hn_ref):
    o = o_ref[...].astype(F32)
    h = h_ref[...] + o * _rms_scale(o) * pg_ref[...]
    ho_ref[...] = h
    hn_ref[...] = (h * _rms_scale(h) * ng_ref[...]).astype(hn_ref.dtype)


def _resid_norm(o, h, post_g, next_g):
    t, d = h.shape
    tm = _tile(t, 256)
    row = pl.BlockSpec((tm, d), lambda i: (i, 0))
    vec = pl.BlockSpec((1, d), lambda i: (0, 0))
    return pl.pallas_call(
        _resid_kernel,
        out_shape=(jax.ShapeDtypeStruct((t, d), F32), jax.ShapeDtypeStruct((t, d), BF16)),
        grid=(t // tm,),
        in_specs=[row, row, vec, vec],
        out_specs=(row, row),
        compiler_params=_params(("parallel",)),
        name="resid_norm",
    )(o, h, post_g.reshape(1, d), next_g.reshape(1, d))


def _final_kernel(o_ref, h_ref, pg_ref, out_ref):
    pg = pg_ref[...]
    for tl in range(TL):
        o = o_ref[tl].astype(F32)
        out_ref[:, tl, :] = h_ref[tl] + o * _rms_scale(o) * pg


def _final_resid(o, h, post_g, batch, seq):
    d = h.shape[1]
    ch = seq // (TL * CL)
    in_spec = pl.BlockSpec((None, TL, None, ch, d), lambda i, j: (i, 0, j, 0, 0))
    out = pl.pallas_call(
        _final_kernel,
        out_shape=jax.ShapeDtypeStruct((batch, ch, CL, TL, d), F32),
        grid=(batch, CL),
        in_specs=[in_spec, in_spec, pl.BlockSpec((1, d), lambda i, j: (0, 0))],
        out_specs=pl.BlockSpec((None, ch, None, TL, d), lambda i, j: (i, 0, j, 0, 0)),
        compiler_params=_params(("parallel", "parallel")),
        name="final_resid",
    )(o.reshape(batch, TL, CL, ch, d), h.reshape(batch, TL, CL, ch, d), post_g.reshape(1, d))
    return out.reshape(batch, seq, d)


def _kv_kernel(mem_ref, g_ref, w_ref, o_ref):
    x = mem_ref[...]
    xn = (x * _rms_scale(x) * g_ref[...]).astype(BF16)
    o_ref[...] = jnp.dot(xn, w_ref[...], preferred_element_type=F32).astype(o_ref.dtype)


def _kv_proj(mem, mem_norm, w_kv):
    b, n, d = mem.shape
    depth, _, n2 = w_kv.shape
    tn = _tile(n2, 512)
    return pl.pallas_call(
        _kv_kernel,
        out_shape=jax.ShapeDtypeStruct((depth, b, n, n2), BF16),
        grid=(depth, n2 // tn, b),
        in_specs=[pl.BlockSpec((None, n, d), lambda l, j, i: (i, 0, 0)),
                  pl.BlockSpec((None, 1, d), lambda l, j, i: (l, 0, 0)),
                  pl.BlockSpec((None, d, tn), lambda l, j, i: (l, 0, j))],
        out_specs=pl.BlockSpec((None, None, n, tn), lambda l, j, i: (l, i, 0, j)),
        compiler_params=_params(("parallel", "parallel", "arbitrary")),
        name="kv_proj",
    )(mem, mem_norm.reshape(depth, 1, d), w_kv)


def _attn_kernel(q_ref, qg_ref, kv_ref, o_ref, *, xa, hd):
    scale = hd ** -0.5
    for h in range(XA_HEADS):
        q = q_ref[:, h * hd:(h + 1) * hd]
        k = kv_ref[:, h * hd:(h + 1) * hd]
        v = kv_ref[:, xa + h * hd:xa + (h + 1) * hd]
        s = lax.dot_general(q, k, (((1,), (1,)), ((), ())), preferred_element_type=F32) * scale
        p = jnp.exp(s - jnp.max(s, axis=-1, keepdims=True))
        denom = jnp.sum(p, axis=-1, keepdims=True)
        o = jnp.dot(p.astype(BF16), v, preferred_element_type=F32) / denom
        qg = qg_ref[:, h * hd:(h + 1) * hd].astype(F32)
        o_ref[:, h * hd:(h + 1) * hd] = (o * _silu(qg)).astype(o_ref.dtype)


def _mem_attention(proj, kv_l, layer, batch, rec, xa):
    t = proj.shape[0]
    rows = t // batch
    n_mem = kv_l.shape[2]
    tm = _tile(rows, 1024)
    per_b = rows // tm
    qcol = (2 * rec) // xa
    hd = xa // XA_HEADS
    return pl.pallas_call(
        functools.partial(_attn_kernel, xa=xa, hd=hd),
        out_shape=jax.ShapeDtypeStruct((t, xa), BF16),
        grid=(batch, per_b),
        in_specs=[pl.BlockSpec((tm, xa), lambda b, i: (b * per_b + i, qcol)),
                  pl.BlockSpec((tm, xa), lambda b, i: (b * per_b + i, qcol + 1)),
                  pl.BlockSpec((None, None, n_mem, 2 * xa), lambda b, i: (layer, b, 0, 0))],
        out_specs=pl.BlockSpec((tm, xa), lambda b, i: (b * per_b + i, 0)),
        compiler_params=_params(("parallel", "arbitrary")),
        name="mem_attn",
    )(proj, proj, kv_l)


def _cmul(s, a1, a2, half):
    return s * a1 + pltpu.roll(s, half, axis=1) * a2


def _shift_rows(x, k, period, fill):
    rows = lax.broadcasted_iota(jnp.int32, x.shape, 0)
    return jnp.where((rows & (period - 1)) < k, fill, pltpu.roll(x, k, axis=0))


def _s5_kernel(u_ref, m_ref, be_ref, cp_ref, lt_ref, d_ref, o_ref, t1_ref, ug_ref, z_ref, l_ref, s_ref, t2_ref,
               *, c, ch, gs, gch, p):
    p2 = 2 * p
    for tl in range(TL):
        t1_ref[tl] = u_ref[tl * c:(tl + 1) * c, :].astype(F32).T
    for g in range(gs):
        vt = t1_ref[:, g * gch:(g + 1) * gch, :].reshape(TL * gch, c)
        ug = vt.T.astype(BF16)
        ug_ref[g] = ug
        z_ref[g] = jnp.dot(ug, be_ref[g], preferred_element_type=F32)

    def ratio(row):
        a1 = jnp.broadcast_to(lt_ref[:, row:row + 1, :], (gs, ch, p2)).reshape(gs * ch, p2)
        a2 = jnp.broadcast_to(lt_ref[:, row + 1:row + 2, :], (gs, ch, p2)).reshape(gs * ch, p2)
        return a1, a2

    def zslab(ref, cl):
        return ref[:, cl * ch:(cl + 1) * ch, :].reshape(gs * ch, p2)

    a1, a2 = ratio(0)
    lc = zslab(z_ref, 0)
    l_ref[:, 0:ch, :] = lc.reshape(gs, ch, p2)
    for cl in range(1, CL):
        lc = _cmul(lc, a1, a2, p) + zslab(z_ref, cl)
        l_ref[:, cl * ch:(cl + 1) * ch, :] = lc.reshape(gs, ch, p2)
    tot = lc
    k, row = 1, 2 + 2 * CL
    while k < ch:
        b1, b2 = ratio(row)
        tot = tot + _cmul(_shift_rows(tot, k, ch, 0.0), b1, b2, p)
        k, row = 2 * k, row + 2
    cy = _shift_rows(tot, 1, ch, 0.0)
    s_ref[:, 0:ch, :] = cy.reshape(gs, ch, p2).astype(s_ref.dtype)
    for cl in range(1, CL):
        b1, b2 = ratio(2 + 2 * cl)
        s_in = zslab(l_ref, cl - 1) + _cmul(cy, b1, b2, p)
        s_ref[:, cl * ch:(cl + 1) * ch, :] = s_in.reshape(gs, ch, p2).astype(s_ref.dtype)

    for g in range(gs):
        ug = ug_ref[g]
        y = jnp.dot(ug, m_ref[g], preferred_element_type=F32)
        y = y + jnp.dot(s_ref[g], cp_ref[g], preferred_element_type=F32)
        y = y + d_ref[g] * ug.astype(F32)
        t2_ref[:, g * gch:(g + 1) * gch, :] = _gelu_tanh(y).T.reshape(TL, gch, c)
    for tl in range(TL):
        o_ref[tl * c:(tl + 1) * c, :] = t2_ref[tl].T.astype(o_ref.dtype)


def _s5_tables(lam_re, lam_im, log_step, b_re, b_im, c_re, c_im, d_skip, ch):
    g, p, c = b_re.shape
    hi = lax.Precision.HIGHEST
    lr = jnp.minimum(lam_re.astype(F32), -1e-4)
    li = lam_im.astype(F32)
    dt = jnp.exp(log_step.astype(F32))[:, None]

    def lam_pow(k):
        k = jnp.asarray(k, F32)[..., None, None]
        mag = jnp.exp(lr * dt * k)
        return mag * jnp.cos(li * dt * k), mag * jnp.sin(li * dt * k)

    ab_re, ab_im = lam_pow(1.0)
    den = lr * lr + li * li
    nr, ni = ab_re - 1.0, ab_im
    coef_re = (nr * lr + ni * li) / den
    coef_im = (ni * lr - nr * li) / den
    br, bi = b_re.astype(F32), b_im.astype(F32)
    bb_re = coef_re[..., None] * br - coef_im[..., None] * bi
    bb_im = coef_re[..., None] * bi + coef_im[..., None] * br
    cr, ci = c_re.astype(F32), c_im.astype(F32)

    steps = jnp.arange(TL + 1)
    pr, pi = lam_pow(steps)
    cl_re = cr[None] * pr[:, :, None, :] - ci[None] * pi[:, :, None, :]
    cl_im = cr[None] * pi[:, :, None, :] + ci[None] * pr[:, :, None, :]
    klag = (jnp.einsum('kgop,gpi->gkoi', cl_re[:TL], bb_re, precision=hi)
            - jnp.einsum('kgop,gpi->gkoi', cl_im[:TL], bb_im, precision=hi))
    t_in = jnp.arange(TL)[:, None]
    t_out = jnp.arange(TL)[None, :]
    lag = t_out - t_in
    m = jnp.where((lag >= 0)[None, :, :, None, None], klag[:, jnp.maximum(lag, 0)], 0.0)
    m = m.transpose(0, 1, 4, 2, 3).reshape(g, TL * c, TL * c)

    qr, qi = pr[TL - 1 - jnp.arange(TL)], pi[TL - 1 - jnp.arange(TL)]
    be_re = qr[..., None] * bb_re[None] - qi[..., None] * bb_im[None]
    be_im = qr[..., None] * bb_im[None] + qi[..., None] * bb_re[None]
    bend = jnp.concatenate([be_re, be_im], axis=2)
    bend = bend.transpose(1, 0, 3, 2).reshape(g, TL * c, 2 * p)

    cp = jnp.concatenate([cl_re[1:], -cl_im[1:]], axis=3)
    cpow = cp.transpose(1, 3, 0, 2).reshape(g, 2 * p, TL * c)

    ks = [float(TL)] + [float(TL * cl) for cl in range(CL)]
    k = 1
    while k < ch:
        ks.append(float(TL * CL * k))
        k *= 2
    wr, wi = lam_pow(jnp.array(ks, F32))
    rows = jnp.stack([jnp.concatenate([wr, wr], axis=-1),
                      jnp.concatenate([-wi, wi], axis=-1)], axis=1)
    ltab = rows.reshape(2 * len(ks), g, 2 * p).transpose(1, 0, 2)
    pad = (-ltab.shape[1]) % 8
    ltab = jnp.pad(ltab, ((0, 0), (0, pad), (0, 0)))
    dtile = jnp.tile(d_skip.astype(F32).reshape(g, 1, c), (1, 1, TL))
    return m.astype(BF16), bend.astype(BF16), cpow.astype(BF16), ltab, dtile


def _s5_ssm(proj, tables, batch, rec):
    m, bend, cpow, ltab, dtile = tables
    t = proj.shape[0]
    rows = t // batch
    c = rows // TL
    g, w, p2 = bend.shape
    gch = w // TL
    gs = LANE_SLAB // gch
    nt = ltab.shape[1]
    wspec = lambda *shape: pl.BlockSpec((gs,) + shape, lambda j, b: (j, 0, 0))
    return pl.pallas_call(
        functools.partial(_s5_kernel, c=c, ch=c // CL, gs=gs, gch=gch, p=p2 // 2),
        out_shape=jax.ShapeDtypeStruct((t, rec), BF16),
        grid=(g // gs, batch),
        in_specs=[pl.BlockSpec((rows, LANE_SLAB), lambda j, b: (b, j)),
                  wspec(w, w), wspec(w, p2), wspec(p2, w), wspec(nt, p2), wspec(1, w)],
        out_specs=pl.BlockSpec((rows, LANE_SLAB), lambda j, b: (b, j)),
        scratch_shapes=[pltpu.VMEM((TL, LANE_SLAB, c), F32),
                        pltpu.VMEM((gs, c, w), BF16),
                        pltpu.VMEM((gs, c, p2), F32),
                        pltpu.VMEM((gs, c, p2), F32),
                        pltpu.VMEM((gs, c, p2), BF16),
                        pltpu.VMEM((TL, LANE_SLAB, c), F32)],
        compiler_params=_params(("parallel", "arbitrary")),
        name="s5_ssm",
    )(proj, m, bend, cpow, ltab, dtile)


def _glu_kernel(g_ref, w_ref, b_ref, gn_ref, gate_ref, o_ref):
    z = jnp.dot(g_ref[...], w_ref[...], preferred_element_type=F32) + b_ref[...]
    gn = gn_ref[...].astype(F32)
    gate = gate_ref[...].astype(F32)
    o_ref[...] = (gn * jax.nn.sigmoid(z) * _silu(gate)).astype(o_ref.dtype)


def _glu_gate(gact, w_glu, b_glu, proj):
    t, rec = gact.shape
    tm, tn = _tile(t, 1024), _tile(rec, 1024)
    goff = rec // tn
    return pl.pallas_call(
        _glu_kernel,
        out_shape=jax.ShapeDtypeStruct((t, rec), BF16),
        grid=(t // tm, rec // tn),
        in_specs=[pl.BlockSpec((tm, rec), lambda i, j: (i, 0)),
                  pl.BlockSpec((rec, tn), lambda i, j: (0, j)),
                  pl.BlockSpec((1, tn), lambda i, j: (0, j)),
                  pl.BlockSpec((tm, tn), lambda i, j: (i, j)),
                  pl.BlockSpec((tm, tn), lambda i, j: (i, goff + j))],
        out_specs=pl.BlockSpec((tm, tn), lambda i, j: (i, j)),
        compiler_params=_params(("parallel", "arbitrary")),
        name="s5_glu",
    )(gact, w_glu, b_glu.reshape(1, rec), gact, proj)


def _prev_chunk(x, ch):
    rows = lax.broadcasted_iota(jnp.int32, x.shape, 0)
    wrapped = jnp.where(rows == 0, 0.0, pltpu.roll(x, ch + 1, axis=0))
    return jnp.where(rows < ch, wrapped, pltpu.roll(x, ch, axis=0))


def _lru_kernel(u_ref, gate_ref, cw_ref, cb_ref, wa_ref, ba_ref, wx_ref, bx_ref, lam_ref, o_ref,
                hs_ref, ps_ref, *, c, ch):
    neg_lam = -lam_ref[...]
    sp = jnp.maximum(neg_lam, 0.0) + jnp.log1p(jnp.exp(-jnp.abs(neg_lam)))
    cw = cw_ref[...]
    cb = cb_ref[...]
    wa, wx = wa_ref[...], wx_ref[...]
    ba, bx = ba_ref[...], bx_ref[...]
    conv_w = cw.shape[0]

    def slab(tl):
        return u_ref[tl * c:(tl + 1) * c, :].astype(F32)

    h = None
    pr = None
    for tl in range(TL):
        xc = cb + cw[conv_w - 1:conv_w] * slab(tl)
        for j in range(1, conv_w):
            wj = cw[conv_w - 1 - j:conv_w - j]
            if tl - j >= 0:
                xc = xc + wj * slab(tl - j)
            else:
                xc = xc + wj * _prev_chunk(slab(tl - j + TL), ch)
        xcb = xc.astype(BF16)
        r = jax.nn.sigmoid(jnp.dot(xcb, wa, preferred_element_type=F32) + ba)
        ig = jax.nn.sigmoid(jnp.dot(xcb, wx, preferred_element_type=F32) + bx)
        log_a = (-LRU_C) * r * sp
        a = jnp.exp(log_a)
        mult = jnp.sqrt(1.0 - a * a)
        if tl == 0:
            rows = lax.broadcasted_iota(jnp.int32, mult.shape, 0)
            mult = jnp.where(rows == 0, 1.0, mult)
        bt = mult * (ig * xc)
        if tl == 0:
            h, pr = bt, a
        else:
            h, pr = a * h + bt, a * pr
        hs_ref[tl * c:(tl + 1) * c, :] = h
        ps_ref[tl * c:(tl + 1) * c, :] = pr
    av, bv = pr[0:ch], h[0:ch]
    a_pre, b_pre = [av], [bv]
    for cl in range(1, CL):
        pc, hc = pr[cl * ch:(cl + 1) * ch], h[cl * ch:(cl + 1) * ch]
        av, bv = pc * av, pc * bv + hc
        a_pre.append(av)
        b_pre.append(bv)
    k = 1
    while k < ch:
        bv = bv + av * _shift_rows(bv, k, ch, 0.0)
        av = av * _shift_rows(av, k, ch, 1.0)
        k *= 2
    cy = _shift_rows(bv, 1, ch, 0.0)
    carry = jnp.concatenate([cy] + [a_pre[cl - 1] * cy + b_pre[cl - 1] for cl in range(1, CL)], axis=0)
    for tl in range(TL):
        hf = hs_ref[tl * c:(tl + 1) * c, :] + ps_ref[tl * c:(tl + 1) * c, :] * carry
        gate = gate_ref[tl * c:(tl + 1) * c, :].astype(F32)
        o_ref[tl * c:(tl + 1) * c, :] = (hf * _silu(gate)).astype(o_ref.dtype)


def _rglru(proj, conv_w, conv_b, w_a, b_a, w_x, b_x, lam, batch, rec):
    t = proj.shape[0]
    rows = t // batch
    nblk, blk, _ = w_a.shape
    cwid = conv_w.shape[0]
    goff = rec // blk
    c = rows // TL
    vec = lambda v: v.astype(F32).reshape(nblk, 1, blk)
    return pl.pallas_call(
        functools.partial(_lru_kernel, c=c, ch=c // CL),
        out_shape=jax.ShapeDtypeStruct((t, rec), BF16),
        grid=(batch, nblk),
        in_specs=[pl.BlockSpec((rows, blk), lambda b, j: (b, j)),
                  pl.BlockSpec((rows, blk), lambda b, j: (b, goff + j)),
                  pl.BlockSpec((cwid, blk), lambda b, j: (0, j)),
                  pl.BlockSpec((None, 1, blk), lambda b, j: (j, 0, 0)),
                  pl.BlockSpec((None, blk, blk), lambda b, j: (j, 0, 0)),
                  pl.BlockSpec((None, 1, blk), lambda b, j: (j, 0, 0)),
                  pl.BlockSpec((None, blk, blk), lambda b, j: (j, 0, 0)),
                  pl.BlockSpec((None, 1, blk), lambda b, j: (j, 0, 0)),
                  pl.BlockSpec((None, 1, blk), lambda b, j: (j, 0, 0))],
        out_specs=pl.BlockSpec((rows, blk), lambda b, j: (b, j)),
        scratch_shapes=[pltpu.VMEM((rows, blk), F32), pltpu.VMEM((rows, blk), F32)],
        compiler_params=_params(("parallel", "arbitrary")),
        name="rglru",
    )(proj, proj, conv_w.astype(F32), vec(conv_b), w_a.astype(BF16), vec(b_a),
      w_x.astype(BF16), vec(b_x), vec(lam))


def kernel(x, mem, w_in, w_kv, w_out, pre_norm, post_norm, mem_norm, s5_lam_re, s5_lam_im, s5_log_step, s5_b_re, s5_b_im, s5_c_re, s5_c_im, s5_d, s5_w_glu, s5_b_glu, lru_conv_w, lru_conv_b, lru_w_a, lru_b_a, lru_w_x, lru_b_x, lru_lam):
    batch, seq, d = x.shape
    depth = w_in.shape[0]
    xa = w_kv.shape[2] // 2
    rec = w_out.shape[1] - xa
    groups, _, gch = s5_b_re.shape[1:]
    assert seq % (TL * CL) == 0 and TL * gch == LANE_SLAB and rec == groups * gch
    ch = seq // (TL * CL)
    assert ch & (ch - 1) == 0, "row-shift masks assume a power-of-two super-chunk count"

    kv = _kv_proj(mem, mem_norm, w_kv.astype(BF16))
    h, hn = _entry(x, pre_norm[0])

    for i in range(depth):
        j = i // 2
        proj = _in_proj(hn, w_in[i].astype(BF16))
        if i % 2 == 0:
            tables = _s5_tables(s5_lam_re[j], s5_lam_im[j], s5_log_step[j], s5_b_re[j], s5_b_im[j],
                                s5_c_re[j], s5_c_im[j], s5_d[j], ch)
            gact = _s5_ssm(proj, tables, batch, rec)
            ymix = _glu_gate(gact, s5_w_glu[j].astype(BF16), s5_b_glu[j].astype(F32), proj)
        else:
            ymix = _rglru(proj, lru_conv_w[j], lru_conv_b[j], lru_w_a[j], lru_b_a[j],
                          lru_w_x[j], lru_b_x[j], lru_lam[j], batch, rec)
        mmix = _mem_attention(proj, kv, i, batch, rec, xa)
        o = _out_proj(ymix, mmix, w_out[i].astype(BF16))
        if i + 1 < depth:
            h, hn = _resid_norm(o, h, post_norm[i], pre_norm[i + 1])
        else:
            h = _final_resid(o, h, post_norm[i], batch, seq)
    return h
```

```python
import functools
import math

import jax
import jax.numpy as jnp
from jax import lax
from jax.experimental import pallas as pl
from jax.experimental.pallas import tpu as pltpu

F32 = jnp.float32
BF16 = jnp.bfloat16

EPS = 1e-6
XA_HEADS = 4
LRU_C = 8.0
TL = 16
CL = 16
LANE_SLAB = 256
VMEM_LIMIT_V7X = 56 * 1024 * 1024


def _tile(n, pref):
    if n <= pref:
        return n
    t = (pref // 128) * 128
    while n % t:
        t -= 128
    return t


def _params(semantics):
    return pltpu.CompilerParams(dimension_semantics=semantics, vmem_limit_bytes=VMEM_LIMIT_V7X)


def _silu(x):
    return x * jax.nn.sigmoid(x)


def _gelu_tanh(x):
    c = math.sqrt(2.0 / math.pi)
    return 0.5 * x * (1.0 + jnp.tanh(c * (x + 0.044715 * (x * x * x))))


def _rms_scale(x):
    return lax.rsqrt(jnp.mean(x * x, axis=-1, keepdims=True) + EPS)


def _time_spec(ch, d):
    return pl.BlockSpec((None, ch, None, TL, d), lambda i, j: (i, 0, j, 0, 0))


def _trunk_spec(ch, d):
    return pl.BlockSpec((None, TL, None, ch, d), lambda i, j: (i, 0, j, 0, 0))


def _entry_kernel(x_ref, g_ref, hn_ref, t_ref):
    g = g_ref[...]
    for ch in range(x_ref.shape[0]):
        t_ref[:, ch, :] = x_ref[ch]
    for tl in range(TL):
        x = t_ref[tl]
        hn_ref[tl] = (x * _rms_scale(x) * g).astype(hn_ref.dtype)


def _entry(x, g):
    b, seq, d = x.shape
    ch = seq // (TL * CL)
    hn = pl.pallas_call(
        _entry_kernel,
        out_shape=jax.ShapeDtypeStruct((b, TL, CL, ch, d), BF16),
        grid=(b, CL),
        in_specs=[_time_spec(ch, d), pl.BlockSpec((1, d), lambda i, j: (0, 0))],
        out_specs=_trunk_spec(ch, d),
        scratch_shapes=[pltpu.VMEM((TL, ch, d), F32)],
        compiler_params=_params(("parallel", "parallel")),
        name="entry_norm",
    )(x.reshape(b, ch, CL, TL, d), g.reshape(1, d))
    return hn.reshape(b * seq, d)


def _first_resid_kernel(o_ref, x_ref, pg_ref, ng_ref, ho_ref, hn_ref):
    pg, ng = pg_ref[...], ng_ref[...]
    for ch in range(x_ref.shape[0]):
        ho_ref[:, ch, :] = x_ref[ch]
    for tl in range(TL):
        o = o_ref[tl].astype(F32)
        h = ho_ref[tl] + o * _rms_scale(o) * pg
        ho_ref[tl] = h
        hn_ref[tl] = (h * _rms_scale(h) * ng).astype(hn_ref.dtype)


def _first_resid_norm(o, x, post_g, next_g):
    b, seq, d = x.shape
    ch = seq // (TL * CL)
    vec = pl.BlockSpec((1, d), lambda i, j: (0, 0))
    h, hn = pl.pallas_call(
        _first_resid_kernel,
        out_shape=(jax.ShapeDtypeStruct((b, TL, CL, ch, d), F32),
                   jax.ShapeDtypeStruct((b, TL, CL, ch, d), BF16)),
        grid=(b, CL),
        in_specs=[_trunk_spec(ch, d), _time_spec(ch, d), vec, vec],
        out_specs=(_trunk_spec(ch, d), _trunk_spec(ch, d)),
        compiler_params=_params(("parallel", "parallel")),
        name="first_resid",
    )(o.reshape(b, TL, CL, ch, d), x.reshape(b, ch, CL, TL, d), post_g.reshape(1, d), next_g.reshape(1, d))
    return h.reshape(b * seq, d), hn.reshape(b * seq, d)


def _matmul_kernel(x_ref, w_ref, o_ref):
    o_ref[...] = jnp.dot(x_ref[...], w_ref[...], preferred_element_type=F32).astype(o_ref.dtype)


def _in_proj(hn, w_all, layer):
    t, d = hn.shape
    n = w_all.shape[2]
    tm, tn = _tile(t, 1024), _tile(n, 1024)
    return pl.pallas_call(
        _matmul_kernel,
        out_shape=jax.ShapeDtypeStruct((t, n), BF16),
        grid=(t // tm, n // tn),
        in_specs=[pl.BlockSpec((tm, d), lambda i, j: (i, 0)),
                  pl.BlockSpec((None, d, tn), lambda i, j: (layer, 0, j))],
        out_specs=pl.BlockSpec((tm, tn), lambda i, j: (i, j)),
        compiler_params=_params(("parallel", "arbitrary")),
        name="in_proj",
    )(hn, w_all)


def _out_matmul_kernel(y_ref, m_ref, w_ref, o_ref, *, rec):
    acc = jnp.dot(y_ref[...], w_ref[0:rec, :], preferred_element_type=F32)
    acc = acc + jnp.dot(m_ref[...], w_ref[rec:, :], preferred_element_type=F32)
    o_ref[...] = acc.astype(o_ref.dtype)


def _out_proj(ymix, mmix, w_all, layer):
    t, rec = ymix.shape
    xa = mmix.shape[1]
    d = w_all.shape[2]
    tm, tn = _tile(t, 1024), _tile(d, 1024)
    return pl.pallas_call(
        functools.partial(_out_matmul_kernel, rec=rec),
        out_shape=jax.ShapeDtypeStruct((t, d), BF16),
        grid=(t // tm, d // tn),
        in_specs=[pl.BlockSpec((tm, rec), lambda i, j: (i, 0)),
                  pl.BlockSpec((tm, xa), lambda i, j: (i, 0)),
                  pl.BlockSpec((None, rec + xa, tn), lambda i, j: (layer, 0, j))],
        out_specs=pl.BlockSpec((tm, tn), lambda i, j: (i, j)),
        compiler_params=_params(("parallel", "arbitrary")),
        name="out_proj",
    )(ymix, mmix, w_all)


def _resid_kernel(o_ref, h_ref, pg_ref, ng_ref, ho_ref, hn_ref):
    o = o_ref[...].astype(F32)
    h = h_ref[...] + o * _rms_scale(o) * pg_ref[...]
    ho_ref[...] = h
    hn_ref[...] = (h * _rms_scale(h) * ng_ref[...]).astype(hn_ref.dtype)


def _resid_norm(o, h, post_g, next_g):
    t, d = h.shape
    tm = _tile(t, 256)
    row = pl.BlockSpec((tm, d), lambda i: (i, 0))
    vec = pl.BlockSpec((1, d), lambda i: (0, 0))
    return pl.pallas_call(
        _resid_kernel,
        out_shape=(jax.ShapeDtypeStruct((t, d), F32), jax.ShapeDtypeStruct((t, d), BF16)),
        grid=(t // tm,),
        in_specs=[row, row, vec, vec],
        out_specs=(row, row),
        compiler_params=_params(("parallel",)),
        name="resid_norm",
    )(o, h, post_g.reshape(1, d), next_g.reshape(1, d))


def _final_kernel(o_ref, h_ref, pg_ref, out_ref):
    pg = pg_ref[...]
    for tl in range(TL):
        o = o_ref[tl].astype(F32)
        out_ref[:, tl, :] = h_ref[tl] + o * _rms_scale(o) * pg


def _final_resid(o, h, post_g, batch, seq):
    d = h.shape[1]
    ch = seq // (TL * CL)
    in_spec = pl.BlockSpec((None, TL, None, ch, d), lambda i, j: (i, 0, j, 0, 0))
    out = pl.pallas_call(
        _final_kernel,
        out_shape=jax.ShapeDtypeStruct((batch, ch, CL, TL, d), F32),
        grid=(batch, CL),
        in_specs=[in_spec, in_spec, pl.BlockSpec((1, d), lambda i, j: (0, 0))],
        out_specs=pl.BlockSpec((None, ch, None, TL, d), lambda i, j: (i, 0, j, 0, 0)),
        compiler_params=_params(("parallel", "parallel")),
        name="final_resid",
    )(o.reshape(batch, TL, CL, ch, d), h.reshape(batch, TL, CL, ch, d), post_g.reshape(1, d))
    return out.reshape(batch, seq, d)


def _kv_kernel(mem_ref, g_ref, w_ref, o_ref):
    x = mem_ref[...]
    xn = (x * _rms_scale(x) * g_ref[...]).astype(BF16)
    o_ref[...] = jnp.dot(xn, w_ref[...], preferred_element_type=F32).astype(o_ref.dtype)


def _kv_proj(mem, mem_norm, w_kv):
    b, n, d = mem.shape
    depth, _, n2 = w_kv.shape
    tn = _tile(n2, 512)
    return pl.pallas_call(
        _kv_kernel,
        out_shape=jax.ShapeDtypeStruct((depth, b, n, n2), BF16),
        grid=(depth, n2 // tn, b),
        in_specs=[pl.BlockSpec((None, n, d), lambda l, j, i: (i, 0, 0)),
                  pl.BlockSpec((None, 1, d), lambda l, j, i: (l, 0, 0)),
                  pl.BlockSpec((None, d, tn), lambda l, j, i: (l, 0, j))],
        out_specs=pl.BlockSpec((None, None, n, tn), lambda l, j, i: (l, i, 0, j)),
        compiler_params=_params(("parallel", "parallel", "arbitrary")),
        name="kv_proj",
    )(mem, mem_norm.reshape(depth, 1, d), w_kv)


def _attn_kernel(q_ref, qg_ref, kv_ref, o_ref, *, xa, hd):
    scale = hd ** -0.5
    for h in range(XA_HEADS):
        q = q_ref[:, h * hd:(h + 1) * hd]
        k = kv_ref[:, h * hd:(h + 1) * hd]
        v = kv_ref[:, xa + h * hd:xa + (h + 1) * hd]
        s = lax.dot_general(q, k, (((1,), (1,)), ((), ())), preferred_element_type=F32) * scale
        p = jnp.exp(s - jnp.max(s, axis=-1, keepdims=True))
        denom = jnp.sum(p, axis=-1, keepdims=True)
        o = jnp.dot(p.astype(BF16), v, preferred_element_type=F32) / denom
        qg = qg_ref[:, h * hd:(h + 1) * hd].astype(F32)
        o_ref[:, h * hd:(h + 1) * hd] = (o * _silu(qg)).astype(o_ref.dtype)


def _mem_attention(proj, kv_l, layer, batch, rec, xa):
    t = proj.shape[0]
    rows = t // batch
    n_mem = kv_l.shape[2]
    tm = _tile(rows, 1024)
    per_b = rows // tm
    qcol = (2 * rec) // xa
    hd = xa // XA_HEADS
    return pl.pallas_call(
        functools.partial(_attn_kernel, xa=xa, hd=hd),
        out_shape=jax.ShapeDtypeStruct((t, xa), BF16),
        grid=(batch, per_b),
        in_specs=[pl.BlockSpec((tm, xa), lambda b, i: (b * per_b + i, qcol)),
                  pl.BlockSpec((tm, xa), lambda b, i: (b * per_b + i, qcol + 1)),
                  pl.BlockSpec((None, None, n_mem, 2 * xa), lambda b, i: (layer, b, 0, 0))],
        out_specs=pl.BlockSpec((tm, xa), lambda b, i: (b * per_b + i, 0)),
        compiler_params=_params(("parallel", "arbitrary")),
        name="mem_attn",
    )(proj, proj, kv_l)


def _cmul(s, a1, a2, half):
    return s * a1 + pltpu.roll(s, half, axis=1) * a2


def _shift_rows(x, k, period, fill):
    rows = lax.broadcasted_iota(jnp.int32, x.shape, 0)
    return jnp.where((rows & (period - 1)) < k, fill, pltpu.roll(x, k, axis=0))


def _s5_kernel(u_ref, m_ref, be_ref, cp_ref, lt_ref, d_ref, o_ref, t1_ref, ug_ref, z_ref, l_ref, s_ref, t2_ref,
               *, c, ch, gs, gch, p):
    p2 = 2 * p
    for tl in range(TL):
        t1_ref[tl] = u_ref[tl * c:(tl + 1) * c, :].astype(F32).T
    for g in range(gs):
        vt = t1_ref[:, g * gch:(g + 1) * gch, :].reshape(TL * gch, c)
        ug = vt.T.astype(BF16)
        ug_ref[g] = ug
        z_ref[g] = jnp.dot(ug, be_ref[g], preferred_element_type=F32)

    def ratio(row):
        a1 = jnp.broadcast_to(lt_ref[:, row:row + 1, :], (gs, ch, p2)).reshape(gs * ch, p2)
        a2 = jnp.broadcast_to(lt_ref[:, row + 1:row + 2, :], (gs, ch, p2)).reshape(gs * ch, p2)
        return a1, a2

    def zslab(ref, cl):
        return ref[:, cl * ch:(cl + 1) * ch, :].reshape(gs * ch, p2)

    a1, a2 = ratio(0)
    lc = zslab(z_ref, 0)
    l_ref[:, 0:ch, :] = lc.reshape(gs, ch, p2)
    for cl in range(1, CL):
        lc = _cmul(lc, a1, a2, p) + zslab(z_ref, cl)
        l_ref[:, cl * ch:(cl + 1) * ch, :] = lc.reshape(gs, ch, p2)
    tot = lc
    k, row = 1, 2 + 2 * CL
    while k < ch:
        b1, b2 = ratio(row)
        tot = tot + _cmul(_shift_rows(tot, k, ch, 0.0), b1, b2, p)
        k, row = 2 * k, row + 2
    cy = _shift_rows(tot, 1, ch, 0.0)
    s_ref[:, 0:ch, :] = cy.reshape(gs, ch, p2).astype(s_ref.dtype)
    for cl in range(1, CL):
        b1, b2 = ratio(2 + 2 * cl)
        s_in = zslab(l_ref, cl - 1) + _cmul(cy, b1, b2, p)
        s_ref[:, cl * ch:(cl + 1) * ch, :] = s_in.reshape(gs, ch, p2).astype(s_ref.dtype)

    for g in range(gs):
        ug = ug_ref[g]
        y = jnp.dot(ug, m_ref[g], preferred_element_type=F32)
        y = y + jnp.dot(s_ref[g], cp_ref[g], preferred_element_type=F32)
        y = y + d_ref[g] * ug.astype(F32)
        t2_ref[:, g * gch:(g + 1) * gch, :] = _gelu_tanh(y).T.reshape(TL, gch, c)
    for tl in range(TL):
        o_ref[tl * c:(tl + 1) * c, :] = t2_ref[tl].T.astype(o_ref.dtype)


def _s5_tables(lam_re, lam_im, log_step, b_re, b_im, c_re, c_im, d_skip, ch):
    g, p, c = b_re.shape
    hi = lax.Precision.HIGHEST
    lr = jnp.minimum(lam_re.astype(F32), -1e-4)
    li = lam_im.astype(F32)
    dt = jnp.exp(log_step.astype(F32))[:, None]

    def lam_pow(k):
        k = jnp.asarray(k, F32)[..., None, None]
        mag = jnp.exp(lr * dt * k)
        return mag * jnp.cos(li * dt * k), mag * jnp.sin(li * dt * k)

    ab_re, ab_im = lam_pow(1.0)
    den = lr * lr + li * li
    nr, ni = ab_re - 1.0, ab_im
    coef_re = (nr * lr + ni * li) / den
    coef_im = (ni * lr - nr * li) / den
    br, bi = b_re.astype(F32), b_im.astype(F32)
    bb_re = coef_re[..., None] * br - coef_im[..., None] * bi
    bb_im = coef_re[..., None] * bi + coef_im[..., None] * br
    cr, ci = c_re.astype(F32), c_im.astype(F32)

    steps = jnp.arange(TL + 1)
    pr, pi = lam_pow(steps)
    cl_re = cr[None] * pr[:, :, None, :] - ci[None] * pi[:, :, None, :]
    cl_im = cr[None] * pi[:, :, None, :] + ci[None] * pr[:, :, None, :]
    klag = (jnp.einsum('kgop,gpi->gkoi', cl_re[:TL], bb_re, precision=hi)
            - jnp.einsum('kgop,gpi->gkoi', cl_im[:TL], bb_im, precision=hi))
    lag = jnp.arange(TL)[None, :] - jnp.arange(TL)[:, None]
    sel = (lag[None] == jnp.arange(TL)[:, None, None]).astype(F32)
    m = jnp.einsum('ltu,gloi->gtiuo', sel, klag, precision=hi).reshape(g, TL * c, TL * c)

    qr, qi = pr[TL - 1 - jnp.arange(TL)], pi[TL - 1 - jnp.arange(TL)]
    be_re = qr[..., None] * bb_re[None] - qi[..., None] * bb_im[None]
    be_im = qr[..., None] * bb_im[None] + qi[..., None] * bb_re[None]
    bend = jnp.concatenate([be_re, be_im], axis=2)
    bend = bend.transpose(1, 0, 3, 2).reshape(g, TL * c, 2 * p)

    cp = jnp.concatenate([cl_re[1:], -cl_im[1:]], axis=3)
    cpow = cp.transpose(1, 3, 0, 2).reshape(g, 2 * p, TL * c)

    ks = [float(TL)] + [float(TL * cl) for cl in range(CL)]
    k = 1
    while k < ch:
        ks.append(float(TL * CL * k))
        k *= 2
    wr, wi = lam_pow(jnp.array(ks, F32))
    rows = jnp.stack([jnp.concatenate([wr, wr], axis=-1),
                      jnp.concatenate([-wi, wi], axis=-1)], axis=1)
    ltab = rows.reshape(2 * len(ks), g, 2 * p).transpose(1, 0, 2)
    pad = (-ltab.shape[1]) % 8
    ltab = jnp.pad(ltab, ((0, 0), (0, pad), (0, 0)))
    dtile = jnp.tile(d_skip.astype(F32).reshape(g, 1, c), (1, 1, TL))
    return m.astype(BF16), bend.astype(BF16), cpow.astype(BF16), ltab, dtile


def _s5_ssm(proj, tables, layer, batch, rec):
    m, bend, cpow, ltab, dtile = tables
    t = proj.shape[0]
    rows = t // batch
    c = rows // TL
    _, g, w, p2 = bend.shape
    gch = w // TL
    gs = LANE_SLAB // gch
    nt = ltab.shape[2]
    wspec = lambda *shape: pl.BlockSpec((None, gs) + shape, lambda j, b: (layer, j, 0, 0))
    return pl.pallas_call(
        functools.partial(_s5_kernel, c=c, ch=c // CL, gs=gs, gch=gch, p=p2 // 2),
        out_shape=jax.ShapeDtypeStruct((t, rec), BF16),
        grid=(g // gs, batch),
        in_specs=[pl.BlockSpec((rows, LANE_SLAB), lambda j, b: (b, j)),
                  wspec(w, w), wspec(w, p2), wspec(p2, w), wspec(nt, p2), wspec(1, w)],
        out_specs=pl.BlockSpec((rows, LANE_SLAB), lambda j, b: (b, j)),
        scratch_shapes=[pltpu.VMEM((TL, LANE_SLAB, c), F32),
                        pltpu.VMEM((gs, c, w), BF16),
                        pltpu.VMEM((gs, c, p2), F32),
                        pltpu.VMEM((gs, c, p2), F32),
                        pltpu.VMEM((gs, c, p2), BF16),
                        pltpu.VMEM((TL, LANE_SLAB, c), F32)],
        compiler_params=_params(("parallel", "arbitrary")),
        name="s5_ssm",
    )(proj, m, bend, cpow, ltab, dtile)


def _glu_kernel(g_ref, w_ref, b_ref, gn_ref, gate_ref, o_ref):
    z = jnp.dot(g_ref[...], w_ref[...], preferred_element_type=F32) + b_ref[...]
    gn = gn_ref[...].astype(F32)
    gate = gate_ref[...].astype(F32)
    o_ref[...] = (gn * jax.nn.sigmoid(z) * _silu(gate)).astype(o_ref.dtype)


def _glu_gate(gact, w_all, b_glu, layer, proj):
    t, rec = gact.shape
    tm, tn = _tile(t, 1024), _tile(rec, 1024)
    goff = rec // tn
    return pl.pallas_call(
        _glu_kernel,
        out_shape=jax.ShapeDtypeStruct((t, rec), BF16),
        grid=(t // tm, rec // tn),
        in_specs=[pl.BlockSpec((tm, rec), lambda i, j: (i, 0)),
                  pl.BlockSpec((None, rec, tn), lambda i, j: (layer, 0, j)),
                  pl.BlockSpec((1, tn), lambda i, j: (0, j)),
                  pl.BlockSpec((tm, tn), lambda i, j: (i, j)),
                  pl.BlockSpec((tm, tn), lambda i, j: (i, goff + j))],
        out_specs=pl.BlockSpec((tm, tn), lambda i, j: (i, j)),
        compiler_params=_params(("parallel", "arbitrary")),
        name="s5_glu",
    )(gact, w_all, b_glu.reshape(1, rec), gact, proj)


def _prev_chunk(x, ch):
    rows = lax.broadcasted_iota(jnp.int32, x.shape, 0)
    wrapped = jnp.where(rows == 0, 0.0, pltpu.roll(x, ch + 1, axis=0))
    return jnp.where(rows < ch, wrapped, pltpu.roll(x, ch, axis=0))


def _lru_kernel(u_ref, gate_ref, cw_ref, cb_ref, wa_ref, ba_ref, wx_ref, bx_ref, lam_ref, o_ref,
                hs_ref, ps_ref, uf_ref, *, c, ch):
    neg_lam = -lam_ref[...]
    sp = jnp.maximum(neg_lam, 0.0) + jnp.log1p(jnp.exp(-jnp.abs(neg_lam)))
    cw = cw_ref[...]
    cb = cb_ref[...]
    wa, wx = wa_ref[...], wx_ref[...]
    ba, bx = ba_ref[...], bx_ref[...]
    conv_w = cw.shape[0]
    uf_ref[...] = u_ref[...].astype(F32)

    def slab(tl):
        return uf_ref[tl * c:(tl + 1) * c, :]

    h = None
    pr = None
    for tl in range(TL):
        xc = cb + cw[conv_w - 1:conv_w] * slab(tl)
        for j in range(1, conv_w):
            wj = cw[conv_w - 1 - j:conv_w - j]
            if tl - j >= 0:
                xc = xc + wj * slab(tl - j)
            else:
                xc = xc + wj * _prev_chunk(slab(tl - j + TL), ch)
        xcb = xc.astype(BF16)
        r = jax.nn.sigmoid(jnp.dot(xcb, wa, preferred_element_type=F32) + ba)
        ig = jax.nn.sigmoid(jnp.dot(xcb, wx, preferred_element_type=F32) + bx)
        log_a = (-LRU_C) * r * sp
        a = jnp.exp(log_a)
        mult = jnp.sqrt(1.0 - a * a)
        if tl == 0:
            rows = lax.broadcasted_iota(jnp.int32, mult.shape, 0)
            mult = jnp.where(rows == 0, 1.0, mult)
        bt = mult * (ig * xc)
        if tl == 0:
            h, pr = bt, a
        else:
            h, pr = a * h + bt, a * pr
        hs_ref[tl * c:(tl + 1) * c, :] = h
        ps_ref[tl * c:(tl + 1) * c, :] = pr
    av, bv = pr[0:ch], h[0:ch]
    a_pre, b_pre = [av], [bv]
    for cl in range(1, CL):
        pc, hc = pr[cl * ch:(cl + 1) * ch], h[cl * ch:(cl + 1) * ch]
        av, bv = pc * av, pc * bv + hc
        a_pre.append(av)
        b_pre.append(bv)
    k = 1
    while k < ch:
        bv = bv + av * _shift_rows(bv, k, ch, 0.0)
        av = av * _shift_rows(av, k, ch, 1.0)
        k *= 2
    cy = _shift_rows(bv, 1, ch, 0.0)
    carry = jnp.concatenate([cy] + [a_pre[cl - 1] * cy + b_pre[cl - 1] for cl in range(1, CL)], axis=0)
    for tl in range(TL):
        hf = hs_ref[tl * c:(tl + 1) * c, :] + ps_ref[tl * c:(tl + 1) * c, :] * carry
        gate = gate_ref[tl * c:(tl + 1) * c, :].astype(F32)
        o_ref[tl * c:(tl + 1) * c, :] = (hf * _silu(gate)).astype(o_ref.dtype)


def _rglru(proj, conv_w, conv_b, w_a_all, b_a, w_x_all, b_x, lam, layer, batch, rec):
    t = proj.shape[0]
    rows = t // batch
    _, nblk, blk, _ = w_a_all.shape
    cwid = conv_w.shape[0]
    goff = rec // blk
    c = rows // TL
    vec = lambda v: v.astype(F32).reshape(nblk, 1, blk)
    wspec = pl.BlockSpec((None, None, blk, blk), lambda b, j: (layer, j, 0, 0))
    return pl.pallas_call(
        functools.partial(_lru_kernel, c=c, ch=c // CL),
        out_shape=jax.ShapeDtypeStruct((t, rec), BF16),
        grid=(batch, nblk),
        in_specs=[pl.BlockSpec((rows, blk), lambda b, j: (b, j)),
                  pl.BlockSpec((rows, blk), lambda b, j: (b, goff + j)),
                  pl.BlockSpec((cwid, blk), lambda b, j: (0, j)),
                  pl.BlockSpec((None, 1, blk), lambda b, j: (j, 0, 0)),
                  wspec,
                  pl.BlockSpec((None, 1, blk), lambda b, j: (j, 0, 0)),
                  wspec,
                  pl.BlockSpec((None, 1, blk), lambda b, j: (j, 0, 0)),
                  pl.BlockSpec((None, 1, blk), lambda b, j: (j, 0, 0))],
        out_specs=pl.BlockSpec((rows, blk), lambda b, j: (b, j)),
        scratch_shapes=[pltpu.VMEM((rows, blk), F32)] * 3,
        compiler_params=_params(("parallel", "arbitrary")),
        name="rglru",
    )(proj, proj, conv_w.astype(F32), vec(conv_b), w_a_all, vec(b_a), w_x_all, vec(b_x), vec(lam))


def kernel(x, mem, w_in, w_kv, w_out, pre_norm, post_norm, mem_norm, s5_lam_re, s5_lam_im, s5_log_step, s5_b_re, s5_b_im, s5_c_re, s5_c_im, s5_d, s5_w_glu, s5_b_glu, lru_conv_w, lru_conv_b, lru_w_a, lru_b_a, lru_w_x, lru_b_x, lru_lam):
    batch, seq, d = x.shape
    depth = w_in.shape[0]
    xa = w_kv.shape[2] // 2
    rec = w_out.shape[1] - xa
    groups, _, gch = s5_b_re.shape[1:]
    assert seq % (TL * CL) == 0 and TL * gch == LANE_SLAB and rec == groups * gch
    ch = seq // (TL * CL)
    assert ch & (ch - 1) == 0, "row-shift masks assume a power-of-two super-chunk count"

    assert depth >= 2, "first and last layers use distinct residual kernels"

    w_in_b, w_out_b, w_glu_b = w_in.astype(BF16), w_out.astype(BF16), s5_w_glu.astype(BF16)
    w_a_b, w_x_b = lru_w_a.astype(BF16), lru_w_x.astype(BF16)
    tables = jax.vmap(functools.partial(_s5_tables, ch=ch))(
        s5_lam_re, s5_lam_im, s5_log_step, s5_b_re, s5_b_im, s5_c_re, s5_c_im, s5_d)

    kv = _kv_proj(mem, mem_norm, w_kv.astype(BF16))
    hn = _entry(x, pre_norm[0])
    h = None
    for i in range(depth):
        j = i // 2
        proj = _in_proj(hn, w_in_b, i)
        if i % 2 == 0:
            gact = _s5_ssm(proj, tables, j, batch, rec)
            ymix = _glu_gate(gact, w_glu_b, s5_b_glu[j].astype(F32), j, proj)
        else:
            ymix = _rglru(proj, lru_conv_w[j], lru_conv_b[j], w_a_b, lru_b_a[j],
                          w_x_b, lru_b_x[j], lru_lam[j], j, batch, rec)
        mmix = _mem_attention(proj, kv, i, batch, rec, xa)
        o = _out_proj(ymix, mmix, w_out_b, i)
        if i == 0:
            h, hn = _first_resid_norm(o, x, post_norm[i], pre_norm[i + 1])
        elif i + 1 < depth:
            h, hn = _resid_norm(o, h, post_norm[i], pre_norm[i + 1])
        else:
            h = _final_resid(o, h, post_norm[i], batch, seq)
    return h
```

```python
import functools
import math

import jax
import jax.numpy as jnp
from jax import lax
from jax.experimental import pallas as pl
from jax.experimental.pallas import tpu as pltpu

F32 = jnp.float32
BF16 = jnp.bfloat16

EPS = 1e-6
XA_HEADS = 4
LRU_C = 8.0
TL = 16
CL = 16
LANE_SLAB = 256
VMEM_LIMIT_V7X = 56 * 1024 * 1024


def _tile(n, pref):
    if n <= pref:
        return n
    t = (pref // 128) * 128
    while n % t:
        t -= 128
    return t


def _params(semantics):
    return pltpu.CompilerParams(dimension_semantics=semantics, vmem_limit_bytes=VMEM_LIMIT_V7X)


def _silu(x):
    return x * jax.nn.sigmoid(x)


def _gelu_tanh(x):
    c = math.sqrt(2.0 / math.pi)
    return 0.5 * x * (1.0 + jnp.tanh(c * (x + 0.044715 * (x * x * x))))


def _rms_scale(x):
    return lax.rsqrt(jnp.mean(x * x, axis=-1, keepdims=True) + EPS)


def _time_spec(ch, d):
    return pl.BlockSpec((None, ch, None, TL, d), lambda i, j: (i, 0, j, 0, 0))


def _trunk_spec(ch, d):
    return pl.BlockSpec((None, TL, None, ch, d), lambda i, j: (i, 0, j, 0, 0))


def _entry_kernel(x_ref, g_ref, hn_ref, t_ref):
    g = g_ref[...]
    for ch in range(x_ref.shape[0]):
        t_ref[:, ch, :] = x_ref[ch]
    for tl in range(TL):
        x = t_ref[tl]
        hn_ref[tl] = (x * _rms_scale(x) * g).astype(hn_ref.dtype)


def _entry(x, g):
    b, seq, d = x.shape
    ch = seq // (TL * CL)
    hn = pl.pallas_call(
        _entry_kernel,
        out_shape=jax.ShapeDtypeStruct((b, TL, CL, ch, d), BF16),
        grid=(b, CL),
        in_specs=[_time_spec(ch, d), pl.BlockSpec((1, d), lambda i, j: (0, 0))],
        out_specs=_trunk_spec(ch, d),
        scratch_shapes=[pltpu.VMEM((TL, ch, d), F32)],
        compiler_params=_params(("parallel", "parallel")),
        name="entry_norm",
    )(x.reshape(b, ch, CL, TL, d), g.reshape(1, d))
    return hn.reshape(b * seq, d)


def _first_resid_kernel(o_ref, x_ref, pg_ref, ng_ref, ho_ref, hn_ref):
    pg, ng = pg_ref[...], ng_ref[...]
    for ch in range(x_ref.shape[0]):
        ho_ref[:, ch, :] = x_ref[ch]
    for tl in range(TL):
        o = o_ref[tl].astype(F32)
        h = ho_ref[tl] + o * _rms_scale(o) * pg
        ho_ref[tl] = h
        hn_ref[tl] = (h * _rms_scale(h) * ng).astype(hn_ref.dtype)


def _first_resid_norm(o, x, post_g, next_g):
    b, seq, d = x.shape
    ch = seq // (TL * CL)
    vec = pl.BlockSpec((1, d), lambda i, j: (0, 0))
    h, hn = pl.pallas_call(
        _first_resid_kernel,
        out_shape=(jax.ShapeDtypeStruct((b, TL, CL, ch, d), F32),
                   jax.ShapeDtypeStruct((b, TL, CL, ch, d), BF16)),
        grid=(b, CL),
        in_specs=[_trunk_spec(ch, d), _time_spec(ch, d), vec, vec],
        out_specs=(_trunk_spec(ch, d), _trunk_spec(ch, d)),
        compiler_params=_params(("parallel", "parallel")),
        name="first_resid",
    )(o.reshape(b, TL, CL, ch, d), x.reshape(b, ch, CL, TL, d), post_g.reshape(1, d), next_g.reshape(1, d))
    return h.reshape(b * seq, d), hn.reshape(b * seq, d)


def _in_proj_kernel(x_ref, w_ref, *refs):
    n_cast = (len(refs) - 1) // 2
    o_ref = refs[n_cast]
    o_ref[...] = jnp.dot(x_ref[...], w_ref[...], preferred_element_type=F32).astype(o_ref.dtype)
    for src, dst in zip(refs[:n_cast], refs[n_cast + 1:]):
        dst[...] = src[...].astype(dst.dtype)


def _in_proj(hn, w, casts):
    t, d = hn.shape
    n = w.shape[1]
    tm, tn = _tile(t, 1024), _tile(n, 1024)
    ni, nj = t // tm, n // tn
    steps = ni * nj
    in_specs = [pl.BlockSpec((tm, d), lambda i, j: (i, 0)),
                pl.BlockSpec((d, tn), lambda i, j: (0, j))]
    out_shape = [jax.ShapeDtypeStruct((t, n), BF16)]
    out_specs = [pl.BlockSpec((tm, tn), lambda i, j: (i, j))]
    operands = [hn, w]
    for src, layer in casts:
        _, r, c = src.shape
        strip = r // steps
        assert strip * steps == r and strip % 16 == 0, "bf16 strips must cover whole (16, 128) tiles"
        in_specs.append(pl.BlockSpec((None, strip, c), lambda i, j, layer=layer: (layer, i * nj + j, 0)))
        out_shape.append(jax.ShapeDtypeStruct((r, c), BF16))
        out_specs.append(pl.BlockSpec((strip, c), lambda i, j: (i * nj + j, 0)))
        operands.append(src)
    res = pl.pallas_call(
        _in_proj_kernel,
        out_shape=tuple(out_shape),
        grid=(ni, nj),
        in_specs=in_specs,
        out_specs=tuple(out_specs),
        compiler_params=_params(("parallel", "arbitrary")),
        name="in_proj",
    )(*operands)
    return res[0], list(res[1:])


def _out_matmul_kernel(y_ref, m_ref, w_ref, o_ref, *, rec):
    acc = jnp.dot(y_ref[...], w_ref[0:rec, :], preferred_element_type=F32)
    acc = acc + jnp.dot(m_ref[...], w_ref[rec:, :], preferred_element_type=F32)
    o_ref[...] = acc.astype(o_ref.dtype)


def _out_proj(ymix, mmix, w):
    t, rec = ymix.shape
    xa = mmix.shape[1]
    d = w.shape[1]
    tm, tn = _tile(t, 1024), _tile(d, 1024)
    return pl.pallas_call(
        functools.partial(_out_matmul_kernel, rec=rec),
        out_shape=jax.ShapeDtypeStruct((t, d), BF16),
        grid=(t // tm, d // tn),
        in_specs=[pl.BlockSpec((tm, rec), lambda i, j: (i, 0)),
                  pl.BlockSpec((tm, xa), lambda i, j: (i, 0)),
                  pl.BlockSpec((rec + xa, tn), lambda i, j: (0, j))],
        out_specs=pl.BlockSpec((tm, tn), lambda i, j: (i, j)),
        compiler_params=_params(("parallel", "arbitrary")),
        name="out_proj",
    )(ymix, mmix, w)


def _resid_kernel(o_ref, h_ref, pg_ref, ng_ref, ho_ref, hn_ref):
    o = o_ref[...].astype(F32)
    h = h_ref[...] + o * _rms_scale(o) * pg_ref[...]
    ho_ref[...] = h
    hn_ref[...] = (h * _rms_scale(h) * ng_ref[...]).astype(hn_ref.dtype)


def _resid_norm(o, h, post_g, next_g):
    t, d = h.shape
    tm = _tile(t, 256)
    row = pl.BlockSpec((tm, d), lambda i: (i, 0))
    vec = pl.BlockSpec((1, d), lambda i: (0, 0))
    return pl.pallas_call(
        _resid_kernel,
        out_shape=(jax.ShapeDtypeStruct((t, d), F32), jax.ShapeDtypeStruct((t, d), BF16)),
        grid=(t // tm,),
        in_specs=[row, row, vec, vec],
        out_specs=(row, row),
        compiler_params=_params(("parallel",)),
        name="resid_norm",
    )(o, h, post_g.reshape(1, d), next_g.reshape(1, d))


def _final_kernel(o_ref, h_ref, pg_ref, out_ref):
    pg = pg_ref[...]
    for tl in range(TL):
        o = o_ref[tl].astype(F32)
        out_ref[:, tl, :] = h_ref[tl] + o * _rms_scale(o) * pg


def _final_resid(o, h, post_g, batch, seq):
    d = h.shape[1]
    ch = seq // (TL * CL)
    in_spec = pl.BlockSpec((None, TL, None, ch, d), lambda i, j: (i, 0, j, 0, 0))
    out = pl.pallas_call(
        _final_kernel,
        out_shape=jax.ShapeDtypeStruct((batch, ch, CL, TL, d), F32),
        grid=(batch, CL),
        in_specs=[in_spec, in_spec, pl.BlockSpec((1, d), lambda i, j: (0, 0))],
        out_specs=pl.BlockSpec((None, ch, None, TL, d), lambda i, j: (i, 0, j, 0, 0)),
        compiler_params=_params(("parallel", "parallel")),
        name="final_resid",
    )(o.reshape(batch, TL, CL, ch, d), h.reshape(batch, TL, CL, ch, d), post_g.reshape(1, d))
    return out.reshape(batch, seq, d)


def _kv_kernel(mem_ref, g_ref, w_ref, o_ref):
    x = mem_ref[...]
    xn = (x * _rms_scale(x) * g_ref[...]).astype(BF16)
    o_ref[...] = jnp.dot(xn, w_ref[...], preferred_element_type=F32).astype(o_ref.dtype)


def _kv_proj(mem, mem_norm, w_kv):
    b, n, d = mem.shape
    depth, _, n2 = w_kv.shape
    tn = _tile(n2, 512)
    return pl.pallas_call(
        _kv_kernel,
        out_shape=jax.ShapeDtypeStruct((depth, b, n, n2), BF16),
        grid=(depth, n2 // tn, b),
        in_specs=[pl.BlockSpec((None, n, d), lambda l, j, i: (i, 0, 0)),
                  pl.BlockSpec((None, 1, d), lambda l, j, i: (l, 0, 0)),
                  pl.BlockSpec((None, d, tn), lambda l, j, i: (l, 0, j))],
        out_specs=pl.BlockSpec((None, None, n, tn), lambda l, j, i: (l, i, 0, j)),
        compiler_params=_params(("parallel", "parallel", "arbitrary")),
        name="kv_proj",
    )(mem, mem_norm.reshape(depth, 1, d), w_kv)


def _attn_kernel(q_ref, qg_ref, kv_ref, o_ref, *, xa, hd):
    scale = hd ** -0.5
    for h in range(XA_HEADS):
        q = q_ref[:, h * hd:(h + 1) * hd]
        k = kv_ref[:, h * hd:(h + 1) * hd]
        v = kv_ref[:, xa + h * hd:xa + (h + 1) * hd]
        s = lax.dot_general(q, k, (((1,), (1,)), ((), ())), preferred_element_type=F32) * scale
        p = jnp.exp(s - jnp.max(s, axis=-1, keepdims=True))
        denom = jnp.sum(p, axis=-1, keepdims=True)
        o = jnp.dot(p.astype(BF16), v, preferred_element_type=F32) / denom
        qg = qg_ref[:, h * hd:(h + 1) * hd].astype(F32)
        o_ref[:, h * hd:(h + 1) * hd] = (o * _silu(qg)).astype(o_ref.dtype)


def _mem_attention(proj, kv_l, layer, batch, rec, xa):
    t = proj.shape[0]
    rows = t // batch
    n_mem = kv_l.shape[2]
    tm = _tile(rows, 1024)
    per_b = rows // tm
    qcol = (2 * rec) // xa
    hd = xa // XA_HEADS
    return pl.pallas_call(
        functools.partial(_attn_kernel, xa=xa, hd=hd),
        out_shape=jax.ShapeDtypeStruct((t, xa), BF16),
        grid=(batch, per_b),
        in_specs=[pl.BlockSpec((tm, xa), lambda b, i: (b * per_b + i, qcol)),
                  pl.BlockSpec((tm, xa), lambda b, i: (b * per_b + i, qcol + 1)),
                  pl.BlockSpec((None, None, n_mem, 2 * xa), lambda b, i: (layer, b, 0, 0))],
        out_specs=pl.BlockSpec((tm, xa), lambda b, i: (b * per_b + i, 0)),
        compiler_params=_params(("parallel", "arbitrary")),
        name="mem_attn",
    )(proj, proj, kv_l)


def _cmul(s, a1, a2, half):
    return s * a1 + pltpu.roll(s, half, axis=1) * a2


def _shift_rows(x, k, period, fill):
    rows = lax.broadcasted_iota(jnp.int32, x.shape, 0)
    return jnp.where((rows & (period - 1)) < k, fill, pltpu.roll(x, k, axis=0))


def _s5_kernel(u_ref, m_ref, be_ref, cp_ref, lt_ref, d_ref, o_ref, t1_ref, ug_ref, z_ref, l_ref, s_ref, t2_ref,
               *, c, ch, gs, gch, p):
    p2 = 2 * p
    for tl in range(TL):
        t1_ref[tl] = u_ref[tl * c:(tl + 1) * c, :].astype(F32).T
    for g in range(gs):
        vt = t1_ref[:, g * gch:(g + 1) * gch, :].reshape(TL * gch, c)
        ug = vt.T.astype(BF16)
        ug_ref[g] = ug
        z_ref[g] = jnp.dot(ug, be_ref[g], preferred_element_type=F32)

    def ratio(row):
        a1 = jnp.broadcast_to(lt_ref[:, row:row + 1, :], (gs, ch, p2)).reshape(gs * ch, p2)
        a2 = jnp.broadcast_to(lt_ref[:, row + 1:row + 2, :], (gs, ch, p2)).reshape(gs * ch, p2)
        return a1, a2

    def zslab(ref, cl):
        return ref[:, cl * ch:(cl + 1) * ch, :].reshape(gs * ch, p2)

    a1, a2 = ratio(0)
    lc = zslab(z_ref, 0)
    l_ref[:, 0:ch, :] = lc.reshape(gs, ch, p2)
    for cl in range(1, CL):
        lc = _cmul(lc, a1, a2, p) + zslab(z_ref, cl)
        l_ref[:, cl * ch:(cl + 1) * ch, :] = lc.reshape(gs, ch, p2)
    tot = lc
    k, row = 1, 2 + 2 * CL
    while k < ch:
        b1, b2 = ratio(row)
        tot = tot + _cmul(_shift_rows(tot, k, ch, 0.0), b1, b2, p)
        k, row = 2 * k, row + 2
    cy = _shift_rows(tot, 1, ch, 0.0)
    s_ref[:, 0:ch, :] = cy.reshape(gs, ch, p2).astype(s_ref.dtype)
    for cl in range(1, CL):
        b1, b2 = ratio(2 + 2 * cl)
        s_in = zslab(l_ref, cl - 1) + _cmul(cy, b1, b2, p)
        s_ref[:, cl * ch:(cl + 1) * ch, :] = s_in.reshape(gs, ch, p2).astype(s_ref.dtype)

    for g in range(gs):
        ug = ug_ref[g]
        y = jnp.dot(ug, m_ref[g], preferred_element_type=F32)
        y = y + jnp.dot(s_ref[g], cp_ref[g], preferred_element_type=F32)
        y = y + d_ref[g] * ug.astype(F32)
        t2_ref[:, g * gch:(g + 1) * gch, :] = _gelu_tanh(y).T.reshape(TL, gch, c)
    for tl in range(TL):
        o_ref[tl * c:(tl + 1) * c, :] = t2_ref[tl].T.astype(o_ref.dtype)


def _s5_tables(lam_re, lam_im, log_step, b_re, b_im, c_re, c_im, d_skip, ch):
    g, p, c = b_re.shape
    hi = lax.Precision.HIGHEST
    lr = jnp.minimum(lam_re.astype(F32), -1e-4)
    li = lam_im.astype(F32)
    dt = jnp.exp(log_step.astype(F32))[:, None]

    def lam_pow(k):
        k = jnp.asarray(k, F32)[..., None, None]
        mag = jnp.exp(lr * dt * k)
        return mag * jnp.cos(li * dt * k), mag * jnp.sin(li * dt * k)

    ab_re, ab_im = lam_pow(1.0)
    den = lr * lr + li * li
    nr, ni = ab_re - 1.0, ab_im
    coef_re = (nr * lr + ni * li) / den
    coef_im = (ni * lr - nr * li) / den
    br, bi = b_re.astype(F32), b_im.astype(F32)
    bb_re = coef_re[..., None] * br - coef_im[..., None] * bi
    bb_im = coef_re[..., None] * bi + coef_im[..., None] * br
    cr, ci = c_re.astype(F32), c_im.astype(F32)

    steps = jnp.arange(TL + 1)
    pr, pi = lam_pow(steps)
    cl_re = cr[None] * pr[:, :, None, :] - ci[None] * pi[:, :, None, :]
    cl_im = cr[None] * pi[:, :, None, :] + ci[None] * pr[:, :, None, :]
    klag = (jnp.einsum('kgop,gpi->gkoi', cl_re[:TL], bb_re, precision=hi)
            - jnp.einsum('kgop,gpi->gkoi', cl_im[:TL], bb_im, precision=hi))
    lag = jnp.arange(TL)[None, :] - jnp.arange(TL)[:, None]
    sel = (lag[None] == jnp.arange(TL)[:, None, None]).astype(F32)
    m = jnp.einsum('ltu,gloi->gtiuo', sel, klag, precision=hi).reshape(g, TL * c, TL * c)

    qr, qi = pr[TL - 1 - jnp.arange(TL)], pi[TL - 1 - jnp.arange(TL)]
    be_re = qr[..., None] * bb_re[None] - qi[..., None] * bb_im[None]
    be_im = qr[..., None] * bb_im[None] + qi[..., None] * bb_re[None]
    bend = jnp.concatenate([be_re, be_im], axis=2)
    bend = bend.transpose(1, 0, 3, 2).reshape(g, TL * c, 2 * p)

    cp = jnp.concatenate([cl_re[1:], -cl_im[1:]], axis=3)
    cpow = cp.transpose(1, 3, 0, 2).reshape(g, 2 * p, TL * c)

    ks = [float(TL)] + [float(TL * cl) for cl in range(CL)]
    k = 1
    while k < ch:
        ks.append(float(TL * CL * k))
        k *= 2
    wr, wi = lam_pow(jnp.array(ks, F32))
    rows = jnp.stack([jnp.concatenate([wr, wr], axis=-1),
                      jnp.concatenate([-wi, wi], axis=-1)], axis=1)
    ltab = rows.reshape(2 * len(ks), g, 2 * p).transpose(1, 0, 2)
    pad = (-ltab.shape[1]) % 8
    ltab = jnp.pad(ltab, ((0, 0), (0, pad), (0, 0)))
    dtile = jnp.tile(d_skip.astype(F32).reshape(g, 1, c), (1, 1, TL))
    return m.astype(BF16), bend.astype(BF16), cpow.astype(BF16), ltab, dtile


def _s5_ssm(proj, tables, layer, batch, rec):
    m, bend, cpow, ltab, dtile = tables
    t = proj.shape[0]
    rows = t // batch
    c = rows // TL
    _, g, w, p2 = bend.shape
    gch = w // TL
    gs = LANE_SLAB // gch
    nt = ltab.shape[2]
    wspec = lambda *shape: pl.BlockSpec((None, gs) + shape, lambda j, b: (layer, j, 0, 0))
    return pl.pallas_call(
        functools.partial(_s5_kernel, c=c, ch=c // CL, gs=gs, gch=gch, p=p2 // 2),
        out_shape=jax.ShapeDtypeStruct((t, rec), BF16),
        grid=(g // gs, batch),
        in_specs=[pl.BlockSpec((rows, LANE_SLAB), lambda j, b: (b, j)),
                  wspec(w, w), wspec(w, p2), wspec(p2, w), wspec(nt, p2), wspec(1, w)],
        out_specs=pl.BlockSpec((rows, LANE_SLAB), lambda j, b: (b, j)),
        scratch_shapes=[pltpu.VMEM((TL, LANE_SLAB, c), F32),
                        pltpu.VMEM((gs, c, w), BF16),
                        pltpu.VMEM((gs, c, p2), F32),
                        pltpu.VMEM((gs, c, p2), F32),
                        pltpu.VMEM((gs, c, p2), BF16),
                        pltpu.VMEM((TL, LANE_SLAB, c), F32)],
        compiler_params=_params(("parallel", "arbitrary")),
        name="s5_ssm",
    )(proj, m, bend, cpow, ltab, dtile)


def _glu_kernel(g_ref, w_ref, b_ref, gn_ref, gate_ref, o_ref):
    z = jnp.dot(g_ref[...], w_ref[...], preferred_element_type=F32) + b_ref[...]
    gn = gn_ref[...].astype(F32)
    gate = gate_ref[...].astype(F32)
    o_ref[...] = (gn * jax.nn.sigmoid(z) * _silu(gate)).astype(o_ref.dtype)


def _glu_gate(gact, w_all, b_glu, layer, proj):
    t, rec = gact.shape
    tm, tn = _tile(t, 1024), _tile(rec, 1024)
    goff = rec // tn
    return pl.pallas_call(
        _glu_kernel,
        out_shape=jax.ShapeDtypeStruct((t, rec), BF16),
        grid=(t // tm, rec // tn),
        in_specs=[pl.BlockSpec((tm, rec), lambda i, j: (i, 0)),
                  pl.BlockSpec((None, rec, tn), lambda i, j: (layer, 0, j)),
                  pl.BlockSpec((1, tn), lambda i, j: (0, j)),
                  pl.BlockSpec((tm, tn), lambda i, j: (i, j)),
                  pl.BlockSpec((tm, tn), lambda i, j: (i, goff + j))],
        out_specs=pl.BlockSpec((tm, tn), lambda i, j: (i, j)),
        compiler_params=_params(("parallel", "arbitrary")),
        name="s5_glu",
    )(gact, w_all, b_glu.reshape(1, rec), gact, proj)


def _prev_chunk(x, ch):
    rows = lax.broadcasted_iota(jnp.int32, x.shape, 0)
    wrapped = jnp.where(rows == 0, 0.0, pltpu.roll(x, ch + 1, axis=0))
    return jnp.where(rows < ch, wrapped, pltpu.roll(x, ch, axis=0))


def _lru_kernel(u_ref, gate_ref, cw_ref, cb_ref, wa_ref, ba_ref, wx_ref, bx_ref, lam_ref, o_ref,
                hs_ref, ps_ref, uf_ref, *, c, ch):
    neg_lam = -lam_ref[...]
    sp = jnp.maximum(neg_lam, 0.0) + jnp.log1p(jnp.exp(-jnp.abs(neg_lam)))
    cw = cw_ref[...]
    cb = cb_ref[...]
    wa, wx = wa_ref[...], wx_ref[...]
    ba, bx = ba_ref[...], bx_ref[...]
    conv_w = cw.shape[0]
    uf_ref[...] = u_ref[...].astype(F32)

    def slab(tl):
        return uf_ref[tl * c:(tl + 1) * c, :]

    h = None
    pr = None
    for tl in range(TL):
        xc = cb + cw[conv_w - 1:conv_w] * slab(tl)
        for j in range(1, conv_w):
            wj = cw[conv_w - 1 - j:conv_w - j]
            if tl - j >= 0:
                xc = xc + wj * slab(tl - j)
            else:
                xc = xc + wj * _prev_chunk(slab(tl - j + TL), ch)
        xcb = xc.astype(BF16)
        r = jax.nn.sigmoid(jnp.dot(xcb, wa, preferred_element_type=F32) + ba)
        ig = jax.nn.sigmoid(jnp.dot(xcb, wx, preferred_element_type=F32) + bx)
        log_a = (-LRU_C) * r * sp
        a = jnp.exp(log_a)
        mult = jnp.sqrt(1.0 - a * a)
        if tl == 0:
            rows = lax.broadcasted_iota(jnp.int32, mult.shape, 0)
            mult = jnp.where(rows == 0, 1.0, mult)
        bt = mult * (ig * xc)
        if tl == 0:
            h, pr = bt, a
        else:
            h, pr = a * h + bt, a * pr
        hs_ref[tl * c:(tl + 1) * c, :] = h
        ps_ref[tl * c:(tl + 1) * c, :] = pr
    av, bv = pr[0:ch], h[0:ch]
    a_pre, b_pre = [av], [bv]
    for cl in range(1, CL):
        pc, hc = pr[cl * ch:(cl + 1) * ch], h[cl * ch:(cl + 1) * ch]
        av, bv = pc * av, pc * bv + hc
        a_pre.append(av)
        b_pre.append(bv)
    k = 1
    while k < ch:
        bv = bv + av * _shift_rows(bv, k, ch, 0.0)
        av = av * _shift_rows(av, k, ch, 1.0)
        k *= 2
    cy = _shift_rows(bv, 1, ch, 0.0)
    carry = jnp.concatenate([cy] + [a_pre[cl - 1] * cy + b_pre[cl - 1] for cl in range(1, CL)], axis=0)
    for tl in range(TL):
        hf = hs_ref[tl * c:(tl + 1) * c, :] + ps_ref[tl * c:(tl + 1) * c, :] * carry
        gate = gate_ref[tl * c:(tl + 1) * c, :].astype(F32)
        o_ref[tl * c:(tl + 1) * c, :] = (hf * _silu(gate)).astype(o_ref.dtype)


def _rglru(proj, conv_w, conv_b, w_a_all, b_a, w_x_all, b_x, lam, layer, batch, rec):
    t = proj.shape[0]
    rows = t // batch
    _, nblk, blk, _ = w_a_all.shape
    cwid = conv_w.shape[0]
    goff = rec // blk
    c = rows // TL
    vec = lambda v: v.astype(F32).reshape(nblk, 1, blk)
    wspec = pl.BlockSpec((None, None, blk, blk), lambda b, j: (layer, j, 0, 0))
    return pl.pallas_call(
        functools.partial(_lru_kernel, c=c, ch=c // CL),
        out_shape=jax.ShapeDtypeStruct((t, rec), BF16),
        grid=(batch, nblk),
        in_specs=[pl.BlockSpec((rows, blk), lambda b, j: (b, j)),
                  pl.BlockSpec((rows, blk), lambda b, j: (b, goff + j)),
                  pl.BlockSpec((cwid, blk), lambda b, j: (0, j)),
                  pl.BlockSpec((None, 1, blk), lambda b, j: (j, 0, 0)),
                  wspec,
                  pl.BlockSpec((None, 1, blk), lambda b, j: (j, 0, 0)),
                  wspec,
                  pl.BlockSpec((None, 1, blk), lambda b, j: (j, 0, 0)),
                  pl.BlockSpec((None, 1, blk), lambda b, j: (j, 0, 0))],
        out_specs=pl.BlockSpec((rows, blk), lambda b, j: (b, j)),
        scratch_shapes=[pltpu.VMEM((rows, blk), F32)] * 3,
        compiler_params=_params(("parallel", "arbitrary")),
        name="rglru",
    )(proj, proj, conv_w.astype(F32), vec(conv_b), w_a_all, vec(b_a), w_x_all, vec(b_x), vec(lam))


def kernel(x, mem, w_in, w_kv, w_out, pre_norm, post_norm, mem_norm, s5_lam_re, s5_lam_im, s5_log_step, s5_b_re, s5_b_im, s5_c_re, s5_c_im, s5_d, s5_w_glu, s5_b_glu, lru_conv_w, lru_conv_b, lru_w_a, lru_b_a, lru_w_x, lru_b_x, lru_lam):
    batch, seq, d = x.shape
    depth = w_in.shape[0]
    xa = w_kv.shape[2] // 2
    rec = w_out.shape[1] - xa
    groups, _, gch = s5_b_re.shape[1:]
    assert seq % (TL * CL) == 0 and TL * gch == LANE_SLAB and rec == groups * gch
    ch = seq // (TL * CL)
    assert ch & (ch - 1) == 0, "row-shift masks assume a power-of-two super-chunk count"

    assert depth >= 2, "first and last layers use distinct residual kernels"

    w_in_cur = w_in[0].astype(BF16)
    tables = jax.vmap(functools.partial(_s5_tables, ch=ch))(
        s5_lam_re, s5_lam_im, s5_log_step, s5_b_re, s5_b_im, s5_c_re, s5_c_im, s5_d)
    stacked = [w_kv, s5_w_glu, lru_w_a, lru_w_x]
    flat = lambda w: w.reshape(1, -1, w.shape[-1])

    hn = _entry(x, pre_norm[0])
    h = None
    for i in range(depth):
        j = i // 2
        casts = [(w_out, i)] + ([(w_in, i + 1)] if i + 1 < depth else [])
        if i == 0:
            casts += [(flat(w), 0) for w in stacked]
        proj, cast_out = _in_proj(hn, w_in_cur, casts)
        w_out_cur = cast_out[0]
        if i + 1 < depth:
            w_in_cur = cast_out[1]
        if i == 0:
            w_kv_b, w_glu_b, w_a_b, w_x_b = [c.reshape(w.shape) for c, w in zip(cast_out[2:], stacked)]
            kv = _kv_proj(mem, mem_norm, w_kv_b)
        if i % 2 == 0:
            gact = _s5_ssm(proj, tables, j, batch, rec)
            ymix = _glu_gate(gact, w_glu_b, s5_b_glu[j].astype(F32), j, proj)
        else:
            ymix = _rglru(proj, lru_conv_w[j], lru_conv_b[j], w_a_b, lru_b_a[j],
                          w_x_b, lru_b_x[j], lru_lam[j], j, batch, rec)
        mmix = _mem_attention(proj, kv, i, batch, rec, xa)
        o = _out_proj(ymix, mmix, w_out_cur)
        if i == 0:
            h, hn = _first_resid_norm(o, x, post_norm[i], pre_norm[i + 1])
        elif i + 1 < depth:
            h, hn = _resid_norm(o, h, post_norm[i], pre_norm[i + 1])
        else:
            h = _final_resid(o, h, post_norm[i], batch, seq)
    return h
```

```python
import functools
import math

import jax
import jax.numpy as jnp
from jax import lax
from jax.experimental import pallas as pl
from jax.experimental.pallas import tpu as pltpu

F32 = jnp.float32
BF16 = jnp.bfloat16

EPS = 1e-6
XA_HEADS = 4
LRU_C = 8.0
TL = 16
CL = 16
LANE_SLAB = 256
VMEM_LIMIT_V7X = 56 * 1024 * 1024


def _tile(n, pref):
    if n <= pref:
        return n
    t = (pref // 128) * 128
    while n % t:
        t -= 128
    return t


def _params(semantics):
    return pltpu.CompilerParams(dimension_semantics=semantics, vmem_limit_bytes=VMEM_LIMIT_V7X)


def _silu(x):
    return x * jax.nn.sigmoid(x)


def _gelu_tanh(x):
    c = math.sqrt(2.0 / math.pi)
    return 0.5 * x * (1.0 + jnp.tanh(c * (x + 0.044715 * (x * x * x))))


def _rms_scale(x):
    return lax.rsqrt(jnp.mean(x * x, axis=-1, keepdims=True) + EPS)


def _time_spec(ch, d):
    return pl.BlockSpec((None, ch, None, TL, d), lambda i, j: (i, 0, j, 0, 0))


def _trunk_spec(ch, d):
    return pl.BlockSpec((None, TL, None, ch, d), lambda i, j: (i, 0, j, 0, 0))


def _entry_kernel(x_ref, g_ref, hn_ref, t_ref):
    g = g_ref[...]
    for ch in range(x_ref.shape[0]):
        t_ref[:, ch, :] = x_ref[ch]
    for tl in range(TL):
        x = t_ref[tl]
        hn_ref[tl] = (x * _rms_scale(x) * g).astype(hn_ref.dtype)


def _entry(x, g):
    b, seq, d = x.shape
    ch = seq // (TL * CL)
    hn = pl.pallas_call(
        _entry_kernel,
        out_shape=jax.ShapeDtypeStruct((b, TL, CL, ch, d), BF16),
        grid=(b, CL),
        in_specs=[_time_spec(ch, d), pl.BlockSpec((1, d), lambda i, j: (0, 0))],
        out_specs=_trunk_spec(ch, d),
        scratch_shapes=[pltpu.VMEM((TL, ch, d), F32)],
        compiler_params=_params(("parallel", "parallel")),
        name="entry_norm",
    )(x.reshape(b, ch, CL, TL, d), g.reshape(1, d))
    return hn.reshape(b * seq, d)


def _first_resid_kernel(o_ref, x_ref, pg_ref, ng_ref, ho_ref, hn_ref):
    pg, ng = pg_ref[...], ng_ref[...]
    for ch in range(x_ref.shape[0]):
        ho_ref[:, ch, :] = x_ref[ch]
    for tl in range(TL):
        o = o_ref[tl].astype(F32)
        h = ho_ref[tl] + o * _rms_scale(o) * pg
        ho_ref[tl] = h
        hn_ref[tl] = (h * _rms_scale(h) * ng).astype(hn_ref.dtype)


def _first_resid_norm(o, x, post_g, next_g):
    b, seq, d = x.shape
    ch = seq // (TL * CL)
    vec = pl.BlockSpec((1, d), lambda i, j: (0, 0))
    h, hn = pl.pallas_call(
        _first_resid_kernel,
        out_shape=(jax.ShapeDtypeStruct((b, TL, CL, ch, d), F32),
                   jax.ShapeDtypeStruct((b, TL, CL, ch, d), BF16)),
        grid=(b, CL),
        in_specs=[_trunk_spec(ch, d), _time_spec(ch, d), vec, vec],
        out_specs=(_trunk_spec(ch, d), _trunk_spec(ch, d)),
        compiler_params=_params(("parallel", "parallel")),
        name="first_resid",
    )(o.reshape(b, TL, CL, ch, d), x.reshape(b, ch, CL, TL, d), post_g.reshape(1, d), next_g.reshape(1, d))
    return h.reshape(b * seq, d), hn.reshape(b * seq, d)


def _in_proj_kernel(x_ref, w_ref, *refs):
    n_cast = (len(refs) - 1) // 2
    o_ref = refs[n_cast]
    o_ref[...] = jnp.dot(x_ref[...], w_ref[...], preferred_element_type=F32).astype(o_ref.dtype)
    for src, dst in zip(refs[:n_cast], refs[n_cast + 1:]):
        dst[...] = src[...].astype(dst.dtype)


def _in_proj(hn, w, casts):
    t, d = hn.shape
    n = w.shape[1]
    tm, tn = _tile(t, 1024), _tile(n, 1024)
    ni, nj = t // tm, n // tn
    steps = ni * nj
    in_specs = [pl.BlockSpec((tm, d), lambda i, j: (i, 0)),
                pl.BlockSpec((d, tn), lambda i, j: (0, j))]
    out_shape = [jax.ShapeDtypeStruct((t, n), BF16)]
    out_specs = [pl.BlockSpec((tm, tn), lambda i, j: (i, j))]
    operands = [hn, w]
    for src, layer in casts:
        _, r, c = src.shape
        strip = r // steps
        assert strip * steps == r and strip % 16 == 0, "bf16 strips must cover whole (16, 128) tiles"
        in_specs.append(pl.BlockSpec((None, strip, c), lambda i, j, layer=layer: (layer, i * nj + j, 0)))
        out_shape.append(jax.ShapeDtypeStruct((r, c), BF16))
        out_specs.append(pl.BlockSpec((strip, c), lambda i, j: (i * nj + j, 0)))
        operands.append(src)
    res = pl.pallas_call(
        _in_proj_kernel,
        out_shape=tuple(out_shape),
        grid=(ni, nj),
        in_specs=in_specs,
        out_specs=tuple(out_specs),
        compiler_params=_params(("parallel", "arbitrary")),
        name="in_proj",
    )(*operands)
    return res[0], list(res[1:])


def _out_matmul_kernel(y_ref, m_ref, w_ref, o_ref, *, rec):
    acc = jnp.dot(y_ref[...], w_ref[0:rec, :], preferred_element_type=F32)
    acc = acc + jnp.dot(m_ref[...], w_ref[rec:, :], preferred_element_type=F32)
    o_ref[...] = acc.astype(o_ref.dtype)


def _out_proj(ymix, mmix, w):
    t, rec = ymix.shape
    xa = mmix.shape[1]
    d = w.shape[1]
    tm, tn = _tile(t, 1024), _tile(d, 1024)
    return pl.pallas_call(
        functools.partial(_out_matmul_kernel, rec=rec),
        out_shape=jax.ShapeDtypeStruct((t, d), BF16),
        grid=(t // tm, d // tn),
        in_specs=[pl.BlockSpec((tm, rec), lambda i, j: (i, 0)),
                  pl.BlockSpec((tm, xa), lambda i, j: (i, 0)),
                  pl.BlockSpec((rec + xa, tn), lambda i, j: (0, j))],
        out_specs=pl.BlockSpec((tm, tn), lambda i, j: (i, j)),
        compiler_params=_params(("parallel", "arbitrary")),
        name="out_proj",
    )(ymix, mmix, w)


def _resid_kernel(o_ref, h_ref, pg_ref, ng_ref, ho_ref, hn_ref):
    o = o_ref[...].astype(F32)
    h = h_ref[...] + o * _rms_scale(o) * pg_ref[...]
    ho_ref[...] = h
    hn_ref[...] = (h * _rms_scale(h) * ng_ref[...]).astype(hn_ref.dtype)


def _resid_norm(o, h, post_g, next_g):
    t, d = h.shape
    tm = _tile(t, 256)
    row = pl.BlockSpec((tm, d), lambda i: (i, 0))
    vec = pl.BlockSpec((1, d), lambda i: (0, 0))
    return pl.pallas_call(
        _resid_kernel,
        out_shape=(jax.ShapeDtypeStruct((t, d), F32), jax.ShapeDtypeStruct((t, d), BF16)),
        grid=(t // tm,),
        in_specs=[row, row, vec, vec],
        out_specs=(row, row),
        compiler_params=_params(("parallel",)),
        name="resid_norm",
    )(o, h, post_g.reshape(1, d), next_g.reshape(1, d))


def _final_kernel(o_ref, h_ref, pg_ref, out_ref):
    pg = pg_ref[...]
    for tl in range(TL):
        o = o_ref[tl].astype(F32)
        out_ref[:, tl, :] = h_ref[tl] + o * _rms_scale(o) * pg


def _final_resid(o, h, post_g, batch, seq):
    d = h.shape[1]
    ch = seq // (TL * CL)
    in_spec = pl.BlockSpec((None, TL, None, ch, d), lambda i, j: (i, 0, j, 0, 0))
    out = pl.pallas_call(
        _final_kernel,
        out_shape=jax.ShapeDtypeStruct((batch, ch, CL, TL, d), F32),
        grid=(batch, CL),
        in_specs=[in_spec, in_spec, pl.BlockSpec((1, d), lambda i, j: (0, 0))],
        out_specs=pl.BlockSpec((None, ch, None, TL, d), lambda i, j: (i, 0, j, 0, 0)),
        compiler_params=_params(("parallel", "parallel")),
        name="final_resid",
    )(o.reshape(batch, TL, CL, ch, d), h.reshape(batch, TL, CL, ch, d), post_g.reshape(1, d))
    return out.reshape(batch, seq, d)


def _kv_kernel(mem_ref, g_ref, w_ref, o_ref, xn_ref):
    @pl.when(pl.program_id(2) == 0)
    def _():
        x = mem_ref[...]
        xn_ref[...] = (x * _rms_scale(x) * g_ref[...]).astype(xn_ref.dtype)

    o_ref[...] = jnp.dot(xn_ref[...], w_ref[...], preferred_element_type=F32).astype(o_ref.dtype)


def _kv_proj(mem, mem_norm, w_kv):
    b, n, d = mem.shape
    depth, _, n2 = w_kv.shape
    tn = _tile(n2, 512)
    return pl.pallas_call(
        _kv_kernel,
        out_shape=jax.ShapeDtypeStruct((depth, b, n, n2), BF16),
        grid=(b, depth, n2 // tn),
        in_specs=[pl.BlockSpec((None, n, d), lambda i, l, j: (i, 0, 0)),
                  pl.BlockSpec((None, 1, d), lambda i, l, j: (l, 0, 0)),
                  pl.BlockSpec((None, d, tn), lambda i, l, j: (l, 0, j))],
        out_specs=pl.BlockSpec((None, None, n, tn), lambda i, l, j: (l, i, 0, j)),
        scratch_shapes=[pltpu.VMEM((n, d), BF16)],
        compiler_params=_params(("parallel", "arbitrary", "arbitrary")),
        name="kv_proj",
    )(mem, mem_norm.reshape(depth, 1, d), w_kv)


def _attn_kernel(q_ref, qg_ref, kv_ref, o_ref, *, xa, hd):
    scale = hd ** -0.5
    for h in range(XA_HEADS):
        q = q_ref[:, h * hd:(h + 1) * hd]
        k = kv_ref[:, h * hd:(h + 1) * hd]
        v = kv_ref[:, xa + h * hd:xa + (h + 1) * hd]
        s = lax.dot_general(q, k, (((1,), (1,)), ((), ())), preferred_element_type=F32) * scale
        p = jnp.exp(s - jnp.max(s, axis=-1, keepdims=True))
        denom = jnp.sum(p, axis=-1, keepdims=True)
        o = jnp.dot(p.astype(BF16), v, preferred_element_type=F32) / denom
        qg = qg_ref[:, h * hd:(h + 1) * hd].astype(F32)
        o_ref[:, h * hd:(h + 1) * hd] = (o * _silu(qg)).astype(o_ref.dtype)


def _mem_attention(proj, kv_l, layer, batch, rec, xa):
    t = proj.shape[0]
    rows = t // batch
    n_mem = kv_l.shape[2]
    tm = _tile(rows, 1024)
    per_b = rows // tm
    qcol = (2 * rec) // xa
    hd = xa // XA_HEADS
    return pl.pallas_call(
        functools.partial(_attn_kernel, xa=xa, hd=hd),
        out_shape=jax.ShapeDtypeStruct((t, xa), BF16),
        grid=(batch, per_b),
        in_specs=[pl.BlockSpec((tm, xa), lambda b, i: (b * per_b + i, qcol)),
                  pl.BlockSpec((tm, xa), lambda b, i: (b * per_b + i, qcol + 1)),
                  pl.BlockSpec((None, None, n_mem, 2 * xa), lambda b, i: (layer, b, 0, 0))],
        out_specs=pl.BlockSpec((tm, xa), lambda b, i: (b * per_b + i, 0)),
        compiler_params=_params(("parallel", "arbitrary")),
        name="mem_attn",
    )(proj, proj, kv_l)


def _cmul(s, a1, a2, half):
    return s * a1 + pltpu.roll(s, half, axis=1) * a2


def _shift_rows(x, k, period, fill):
    rows = lax.broadcasted_iota(jnp.int32, x.shape, 0)
    return jnp.where((rows & (period - 1)) < k, fill, pltpu.roll(x, k, axis=0))


def _s5_kernel(u_ref, m_ref, be_ref, cp_ref, lt_ref, d_ref, o_ref, t1_ref, ug_ref, z_ref, l_ref, s_ref, t2_ref,
               *, c, ch, gs, gch, p):
    p2 = 2 * p
    for tl in range(TL):
        t1_ref[tl] = u_ref[tl * c:(tl + 1) * c, :].astype(F32).T
    for g in range(gs):
        vt = t1_ref[:, g * gch:(g + 1) * gch, :].reshape(TL * gch, c)
        ug = vt.T.astype(BF16)
        ug_ref[g] = ug
        z_ref[g] = jnp.dot(ug, be_ref[g], preferred_element_type=F32)

    def ratio(row):
        a1 = jnp.broadcast_to(lt_ref[:, row:row + 1, :], (gs, ch, p2)).reshape(gs * ch, p2)
        a2 = jnp.broadcast_to(lt_ref[:, row + 1:row + 2, :], (gs, ch, p2)).reshape(gs * ch, p2)
        return a1, a2

    def zslab(ref, cl):
        return ref[:, cl * ch:(cl + 1) * ch, :].reshape(gs * ch, p2)

    a1, a2 = ratio(0)
    lc = zslab(z_ref, 0)
    l_ref[:, 0:ch, :] = lc.reshape(gs, ch, p2)
    for cl in range(1, CL):
        lc = _cmul(lc, a1, a2, p) + zslab(z_ref, cl)
        l_ref[:, cl * ch:(cl + 1) * ch, :] = lc.reshape(gs, ch, p2)
    tot = lc
    k, row = 1, 2 + 2 * CL
    while k < ch:
        b1, b2 = ratio(row)
        tot = tot + _cmul(_shift_rows(tot, k, ch, 0.0), b1, b2, p)
        k, row = 2 * k, row + 2
    cy = _shift_rows(tot, 1, ch, 0.0)
    s_ref[:, 0:ch, :] = cy.reshape(gs, ch, p2).astype(s_ref.dtype)
    for cl in range(1, CL):
        b1, b2 = ratio(2 + 2 * cl)
        s_in = zslab(l_ref, cl - 1) + _cmul(cy, b1, b2, p)
        s_ref[:, cl * ch:(cl + 1) * ch, :] = s_in.reshape(gs, ch, p2).astype(s_ref.dtype)

    for g in range(gs):
        ug = ug_ref[g]
        y = jnp.dot(ug, m_ref[g], preferred_element_type=F32)
        y = y + jnp.dot(s_ref[g], cp_ref[g], preferred_element_type=F32)
        y = y + d_ref[g] * ug.astype(F32)
        t2_ref[:, g * gch:(g + 1) * gch, :] = _gelu_tanh(y).T.reshape(TL, gch, c)
    for tl in range(TL):
        o_ref[tl * c:(tl + 1) * c, :] = t2_ref[tl].T.astype(o_ref.dtype)


def _s5_tables(lam_re, lam_im, log_step, b_re, b_im, c_re, c_im, d_skip, ch):
    g, p, c = b_re.shape
    hi = lax.Precision.HIGHEST
    lr = jnp.minimum(lam_re.astype(F32), -1e-4)
    li = lam_im.astype(F32)
    dt = jnp.exp(log_step.astype(F32))[:, None]

    def lam_pow(k):
        k = jnp.asarray(k, F32)[..., None, None]
        mag = jnp.exp(lr * dt * k)
        return mag * jnp.cos(li * dt * k), mag * jnp.sin(li * dt * k)

    ab_re, ab_im = lam_pow(1.0)
    den = lr * lr + li * li
    nr, ni = ab_re - 1.0, ab_im
    coef_re = (nr * lr + ni * li) / den
    coef_im = (ni * lr - nr * li) / den
    br, bi = b_re.astype(F32), b_im.astype(F32)
    bb_re = coef_re[..., None] * br - coef_im[..., None] * bi
    bb_im = coef_re[..., None] * bi + coef_im[..., None] * br
    cr, ci = c_re.astype(F32), c_im.astype(F32)

    steps = jnp.arange(TL + 1)
    pr, pi = lam_pow(steps)
    cl_re = cr[None] * pr[:, :, None, :] - ci[None] * pi[:, :, None, :]
    cl_im = cr[None] * pi[:, :, None, :] + ci[None] * pr[:, :, None, :]
    klag = (jnp.einsum('kgop,gpi->gkoi', cl_re[:TL], bb_re, precision=hi)
            - jnp.einsum('kgop,gpi->gkoi', cl_im[:TL], bb_im, precision=hi))
    kw = klag.transpose(0, 3, 1, 2).reshape(g, c, TL * c).astype(BF16)
    m = jnp.stack([jnp.pad(kw[:, :, :(TL - tin) * c], ((0, 0), (0, 0), (tin * c, 0))) for tin in range(TL)],
                  axis=1).reshape(g, TL * c, TL * c)

    qr, qi = pr[TL - 1 - jnp.arange(TL)], pi[TL - 1 - jnp.arange(TL)]
    be_re = qr[..., None] * bb_re[None] - qi[..., None] * bb_im[None]
    be_im = qr[..., None] * bb_im[None] + qi[..., None] * bb_re[None]
    bend = jnp.concatenate([be_re, be_im], axis=2)
    bend = bend.transpose(1, 0, 3, 2).reshape(g, TL * c, 2 * p)

    cp = jnp.concatenate([cl_re[1:], -cl_im[1:]], axis=3)
    cpow = cp.transpose(1, 3, 0, 2).reshape(g, 2 * p, TL * c)

    ks = [float(TL)] + [float(TL * cl) for cl in range(CL)]
    k = 1
    while k < ch:
        ks.append(float(TL * CL * k))
        k *= 2
    wr, wi = lam_pow(jnp.array(ks, F32))
    rows = jnp.stack([jnp.concatenate([wr, wr], axis=-1),
                      jnp.concatenate([-wi, wi], axis=-1)], axis=1)
    ltab = rows.reshape(2 * len(ks), g, 2 * p).transpose(1, 0, 2)
    pad = (-ltab.shape[1]) % 8
    ltab = jnp.pad(ltab, ((0, 0), (0, pad), (0, 0)))
    dtile = jnp.tile(d_skip.astype(F32).reshape(g, 1, c), (1, 1, TL))
    return m.astype(BF16), bend.astype(BF16), cpow.astype(BF16), ltab, dtile


def _s5_ssm(proj, tables, layer, batch, rec):
    m, bend, cpow, ltab, dtile = tables
    t = proj.shape[0]
    rows = t // batch
    c = rows // TL
    _, g, w, p2 = bend.shape
    gch = w // TL
    gs = LANE_SLAB // gch
    nt = ltab.shape[2]
    wspec = lambda *shape: pl.BlockSpec((None, gs) + shape, lambda j, b: (layer, j, 0, 0))
    return pl.pallas_call(
        functools.partial(_s5_kernel, c=c, ch=c // CL, gs=gs, gch=gch, p=p2 // 2),
        out_shape=jax.ShapeDtypeStruct((t, rec), BF16),
        grid=(g // gs, batch),
        in_specs=[pl.BlockSpec((rows, LANE_SLAB), lambda j, b: (b, j)),
                  wspec(w, w), wspec(w, p2), wspec(p2, w), wspec(nt, p2), wspec(1, w)],
        out_specs=pl.BlockSpec((rows, LANE_SLAB), lambda j, b: (b, j)),
        scratch_shapes=[pltpu.VMEM((TL, LANE_SLAB, c), F32),
                        pltpu.VMEM((gs, c, w), BF16),
                        pltpu.VMEM((gs, c, p2), F32),
                        pltpu.VMEM((gs, c, p2), F32),
                        pltpu.VMEM((gs, c, p2), BF16),
                        pltpu.VMEM((TL, LANE_SLAB, c), F32)],
        compiler_params=_params(("parallel", "arbitrary")),
        name="s5_ssm",
    )(proj, m, bend, cpow, ltab, dtile)


def _glu_kernel(g_ref, w_ref, b_ref, gn_ref, gate_ref, o_ref):
    z = jnp.dot(g_ref[...], w_ref[...], preferred_element_type=F32) + b_ref[...]
    gn = gn_ref[...].astype(F32)
    gate = gate_ref[...].astype(F32)
    o_ref[...] = (gn * jax.nn.sigmoid(z) * _silu(gate)).astype(o_ref.dtype)


def _glu_gate(gact, w_all, b_glu, layer, proj):
    t, rec = gact.shape
    tm, tn = _tile(t, 1024), _tile(rec, 1024)
    goff = rec // tn
    return pl.pallas_call(
        _glu_kernel,
        out_shape=jax.ShapeDtypeStruct((t, rec), BF16),
        grid=(t // tm, rec // tn),
        in_specs=[pl.BlockSpec((tm, rec), lambda i, j: (i, 0)),
                  pl.BlockSpec((None, rec, tn), lambda i, j: (layer, 0, j)),
                  pl.BlockSpec((1, tn), lambda i, j: (0, j)),
                  pl.BlockSpec((tm, tn), lambda i, j: (i, j)),
                  pl.BlockSpec((tm, tn), lambda i, j: (i, goff + j))],
        out_specs=pl.BlockSpec((tm, tn), lambda i, j: (i, j)),
        compiler_params=_params(("parallel", "arbitrary")),
        name="s5_glu",
    )(gact, w_all, b_glu.reshape(1, rec), gact, proj)


def _prev_chunk(x, ch):
    rows = lax.broadcasted_iota(jnp.int32, x.shape, 0)
    wrapped = jnp.where(rows == 0, 0.0, pltpu.roll(x, ch + 1, axis=0))
    return jnp.where(rows < ch, wrapped, pltpu.roll(x, ch, axis=0))


def _lru_kernel(u_ref, gate_ref, cw_ref, cb_ref, wa_ref, ba_ref, wx_ref, bx_ref, lam_ref, o_ref,
                hs_ref, ps_ref, uf_ref, *, c, ch):
    neg_lam = -lam_ref[...]
    sp = jnp.maximum(neg_lam, 0.0) + jnp.log1p(jnp.exp(-jnp.abs(neg_lam)))
    rate = sp * (-LRU_C / math.log(2.0))
    cw = cw_ref[...]
    cb = cb_ref[...]
    wa, wx = wa_ref[...], wx_ref[...]
    ba, bx = ba_ref[...], bx_ref[...]
    conv_w = cw.shape[0]
    uf_ref[...] = u_ref[...].astype(F32)

    def slab(tl):
        return uf_ref[tl * c:(tl + 1) * c, :]

    h = None
    pr = None
    for tl in range(TL):
        xc = cb + cw[conv_w - 1:conv_w] * slab(tl)
        for j in range(1, conv_w):
            wj = cw[conv_w - 1 - j:conv_w - j]
            if tl - j >= 0:
                xc = xc + wj * slab(tl - j)
            else:
                xc = xc + wj * _prev_chunk(slab(tl - j + TL), ch)
        xcb = xc.astype(BF16)
        r = jax.nn.sigmoid(jnp.dot(xcb, wa, preferred_element_type=F32) + ba)
        ig = jax.nn.sigmoid(jnp.dot(xcb, wx, preferred_element_type=F32) + bx)
        a = jnp.exp2(r * rate)
        mult = jnp.sqrt(1.0 - a * a)
        if tl == 0:
            rows = lax.broadcasted_iota(jnp.int32, mult.shape, 0)
            mult = jnp.where(rows == 0, 1.0, mult)
        bt = mult * (ig * xc)
        if tl == 0:
            h, pr = bt, a
        else:
            h, pr = a * h + bt, a * pr
        hs_ref[tl * c:(tl + 1) * c, :] = h
        ps_ref[tl * c:(tl + 1) * c, :] = pr
    av, bv = pr[0:ch], h[0:ch]
    a_pre, b_pre = [av], [bv]
    for cl in range(1, CL):
        pc, hc = pr[cl * ch:(cl + 1) * ch], h[cl * ch:(cl + 1) * ch]
        av, bv = pc * av, pc * bv + hc
        a_pre.append(av)
        b_pre.append(bv)
    k = 1
    while k < ch:
        bv = bv + av * _shift_rows(bv, k, ch, 0.0)
        av = av * _shift_rows(av, k, ch, 1.0)
        k *= 2
    cy = _shift_rows(bv, 1, ch, 0.0)
    carry = jnp.concatenate([cy] + [a_pre[cl - 1] * cy + b_pre[cl - 1] for cl in range(1, CL)], axis=0)
    for tl in range(TL):
        hf = hs_ref[tl * c:(tl + 1) * c, :] + ps_ref[tl * c:(tl + 1) * c, :] * carry
        gate = gate_ref[tl * c:(tl + 1) * c, :].astype(F32)
        o_ref[tl * c:(tl + 1) * c, :] = (hf * _silu(gate)).astype(o_ref.dtype)


def _rglru(proj, conv_w, conv_b, w_a_all, b_a, w_x_all, b_x, lam, layer, batch, rec):
    t = proj.shape[0]
    rows = t // batch
    _, nblk, blk, _ = w_a_all.shape
    cwid = conv_w.shape[0]
    goff = rec // blk
    c = rows // TL
    vec = lambda v: v.astype(F32).reshape(nblk, 1, blk)
    wspec = pl.BlockSpec((None, None, blk, blk), lambda b, j: (layer, j, 0, 0))
    return pl.pallas_call(
        functools.partial(_lru_kernel, c=c, ch=c // CL),
        out_shape=jax.ShapeDtypeStruct((t, rec), BF16),
        grid=(batch, nblk),
        in_specs=[pl.BlockSpec((rows, blk), lambda b, j: (b, j)),
                  pl.BlockSpec((rows, blk), lambda b, j: (b, goff + j)),
                  pl.BlockSpec((cwid, blk), lambda b, j: (0, j)),
                  pl.BlockSpec((None, 1, blk), lambda b, j: (j, 0, 0)),
                  wspec,
                  pl.BlockSpec((None, 1, blk), lambda b, j: (j, 0, 0)),
                  wspec,
                  pl.BlockSpec((None, 1, blk), lambda b, j: (j, 0, 0)),
                  pl.BlockSpec((None, 1, blk), lambda b, j: (j, 0, 0))],
        out_specs=pl.BlockSpec((rows, blk), lambda b, j: (b, j)),
        scratch_shapes=[pltpu.VMEM((rows, blk), F32)] * 3,
        compiler_params=_params(("parallel", "arbitrary")),
        name="rglru",
    )(proj, proj, conv_w.astype(F32), vec(conv_b), w_a_all, vec(b_a), w_x_all, vec(b_x), vec(lam))


def kernel(x, mem, w_in, w_kv, w_out, pre_norm, post_norm, mem_norm, s5_lam_re, s5_lam_im, s5_log_step, s5_b_re, s5_b_im, s5_c_re, s5_c_im, s5_d, s5_w_glu, s5_b_glu, lru_conv_w, lru_conv_b, lru_w_a, lru_b_a, lru_w_x, lru_b_x, lru_lam):
    batch, seq, d = x.shape
    depth = w_in.shape[0]
    xa = w_kv.shape[2] // 2
    rec = w_out.shape[1] - xa
    groups, _, gch = s5_b_re.shape[1:]
    assert seq % (TL * CL) == 0 and TL * gch == LANE_SLAB and rec == groups * gch
    ch = seq // (TL * CL)
    assert ch & (ch - 1) == 0, "row-shift masks assume a power-of-two super-chunk count"

    assert depth >= 2, "first and last layers use distinct residual kernels"

    w_in_cur = w_in[0].astype(BF16)
    tables = jax.vmap(functools.partial(_s5_tables, ch=ch))(
        s5_lam_re, s5_lam_im, s5_log_step, s5_b_re, s5_b_im, s5_c_re, s5_c_im, s5_d)
    stacked = [w_kv, s5_w_glu, lru_w_a, lru_w_x]
    flat = lambda w: w.reshape(1, -1, w.shape[-1])

    hn = _entry(x, pre_norm[0])
    h = None
    for i in range(depth):
        j = i // 2
        casts = [(w_out, i)] + ([(w_in, i + 1)] if i + 1 < depth else [])
        if i == 0:
            casts += [(flat(w), 0) for w in stacked]
        proj, cast_out = _in_proj(hn, w_in_cur, casts)
        w_out_cur = cast_out[0]
        if i + 1 < depth:
            w_in_cur = cast_out[1]
        if i == 0:
            w_kv_b, w_glu_b, w_a_b, w_x_b = [c.reshape(w.shape) for c, w in zip(cast_out[2:], stacked)]
            kv = _kv_proj(mem, mem_norm, w_kv_b)
        if i % 2 == 0:
            gact = _s5_ssm(proj, tables, j, batch, rec)
            ymix = _glu_gate(gact, w_glu_b, s5_b_glu[j].astype(F32), j, proj)
        else:
            ymix = _rglru(proj, lru_conv_w[j], lru_conv_b[j], w_a_b, lru_b_a[j],
                          w_x_b, lru_b_x[j], lru_lam[j], j, batch, rec)
        mmix = _mem_attention(proj, kv, i, batch, rec, xa)
        o = _out_proj(ymix, mmix, w_out_cur)
        if i == 0:
            h, hn = _first_resid_norm(o, x, post_norm[i], pre_norm[i + 1])
        elif i + 1 < depth:
            h, hn = _resid_norm(o, h, post_norm[i], pre_norm[i + 1])
        else:
            h = _final_resid(o, h, post_norm[i], batch, seq)
    return h
```

```python
import functools
import math

import jax
import jax.numpy as jnp
from jax import lax
from jax.experimental import pallas as pl
from jax.experimental.pallas import tpu as pltpu

F32 = jnp.float32
BF16 = jnp.bfloat16

EPS = 1e-6
XA_HEADS = 4
LRU_C = 8.0
TL = 16
CL = 16
LANE_SLAB = 256
VMEM_LIMIT_V7X = 56 * 1024 * 1024


def _tile(n, pref):
    if n <= pref:
        return n
    t = (pref // 128) * 128
    while n % t:
        t -= 128
    return t


def _params(semantics):
    return pltpu.CompilerParams(dimension_semantics=semantics, vmem_limit_bytes=VMEM_LIMIT_V7X)


def _silu(x):
    return x * jax.nn.sigmoid(x)


def _gelu_tanh(x):
    c = math.sqrt(2.0 / math.pi)
    return 0.5 * x * (1.0 + jnp.tanh(c * (x + 0.044715 * (x * x * x))))


def _rms_scale(x):
    return lax.rsqrt(jnp.mean(x * x, axis=-1, keepdims=True) + EPS)


def _time_spec(ch, d):
    return pl.BlockSpec((None, ch, None, TL, d), lambda i, j: (i, 0, j, 0, 0))


def _trunk_spec(ch, d):
    return pl.BlockSpec((None, TL, None, ch, d), lambda i, j: (i, 0, j, 0, 0))


def _entry_kernel(x_ref, g_ref, hn_ref, t_ref):
    g = g_ref[...]
    for ch in range(x_ref.shape[0]):
        t_ref[:, ch, :] = x_ref[ch]
    for tl in range(TL):
        x = t_ref[tl]
        hn_ref[tl] = (x * _rms_scale(x) * g).astype(hn_ref.dtype)


def _entry(x, g):
    b, seq, d = x.shape
    ch = seq // (TL * CL)
    hn = pl.pallas_call(
        _entry_kernel,
        out_shape=jax.ShapeDtypeStruct((b, TL, CL, ch, d), BF16),
        grid=(b, CL),
        in_specs=[_time_spec(ch, d), pl.BlockSpec((1, d), lambda i, j: (0, 0))],
        out_specs=_trunk_spec(ch, d),
        scratch_shapes=[pltpu.VMEM((TL, ch, d), F32)],
        compiler_params=_params(("parallel", "parallel")),
        name="entry_norm",
    )(x.reshape(b, ch, CL, TL, d), g.reshape(1, d))
    return hn.reshape(b * seq, d)


def _first_resid_kernel(o_ref, x_ref, pg_ref, ng_ref, ho_ref, hn_ref):
    pg, ng = pg_ref[...], ng_ref[...]
    for ch in range(x_ref.shape[0]):
        ho_ref[:, ch, :] = x_ref[ch]
    for tl in range(TL):
        o = o_ref[tl].astype(F32)
        h = ho_ref[tl] + o * _rms_scale(o) * pg
        ho_ref[tl] = h
        hn_ref[tl] = (h * _rms_scale(h) * ng).astype(hn_ref.dtype)


def _first_resid_norm(o, x, post_g, next_g):
    b, seq, d = x.shape
    ch = seq // (TL * CL)
    vec = pl.BlockSpec((1, d), lambda i, j: (0, 0))
    h, hn = pl.pallas_call(
        _first_resid_kernel,
        out_shape=(jax.ShapeDtypeStruct((b, TL, CL, ch, d), F32),
                   jax.ShapeDtypeStruct((b, TL, CL, ch, d), BF16)),
        grid=(b, CL),
        in_specs=[_trunk_spec(ch, d), _time_spec(ch, d), vec, vec],
        out_specs=(_trunk_spec(ch, d), _trunk_spec(ch, d)),
        compiler_params=_params(("parallel", "parallel")),
        name="first_resid",
    )(o.reshape(b, TL, CL, ch, d), x.reshape(b, ch, CL, TL, d), post_g.reshape(1, d), next_g.reshape(1, d))
    return h.reshape(b * seq, d), hn.reshape(b * seq, d)


def _in_proj_kernel(x_ref, w_ref, *refs):
    n_cast = (len(refs) - 1) // 2
    o_ref = refs[n_cast]
    o_ref[...] = jnp.dot(x_ref[...], w_ref[...], preferred_element_type=F32).astype(o_ref.dtype)
    for src, dst in zip(refs[:n_cast], refs[n_cast + 1:]):
        dst[...] = src[...].astype(dst.dtype)


def _in_proj(hn, w, casts):
    t, d = hn.shape
    n = w.shape[1]
    tm, tn = _tile(t, 1024), _tile(n, 1024)
    ni, nj = t // tm, n // tn
    steps = ni * nj
    in_specs = [pl.BlockSpec((tm, d), lambda i, j: (i, 0)),
                pl.BlockSpec((d, tn), lambda i, j: (0, j))]
    out_shape = [jax.ShapeDtypeStruct((t, n), BF16)]
    out_specs = [pl.BlockSpec((tm, tn), lambda i, j: (i, j))]
    operands = [hn, w]
    for src, layer in casts:
        _, r, c = src.shape
        strip = r // steps
        assert strip * steps == r and strip % 16 == 0, "bf16 strips must cover whole (16, 128) tiles"
        in_specs.append(pl.BlockSpec((None, strip, c), lambda i, j, layer=layer: (layer, i * nj + j, 0)))
        out_shape.append(jax.ShapeDtypeStruct((r, c), BF16))
        out_specs.append(pl.BlockSpec((strip, c), lambda i, j: (i * nj + j, 0)))
        operands.append(src)
    res = pl.pallas_call(
        _in_proj_kernel,
        out_shape=tuple(out_shape),
        grid=(ni, nj),
        in_specs=in_specs,
        out_specs=tuple(out_specs),
        compiler_params=_params(("parallel", "arbitrary")),
        name="in_proj",
    )(*operands)
    return res[0], list(res[1:])


def _out_matmul_kernel(y_ref, m_ref, w_ref, o_ref, *, rec):
    acc = jnp.dot(y_ref[...], w_ref[0:rec, :], preferred_element_type=F32)
    acc = acc + jnp.dot(m_ref[...], w_ref[rec:, :], preferred_element_type=F32)
    o_ref[...] = acc.astype(o_ref.dtype)


def _out_proj(ymix, mmix, w):
    t, rec = ymix.shape
    xa = mmix.shape[1]
    d = w.shape[1]
    tm, tn = _tile(t, 1024), _tile(d, 1024)
    return pl.pallas_call(
        functools.partial(_out_matmul_kernel, rec=rec),
        out_shape=jax.ShapeDtypeStruct((t, d), BF16),
        grid=(t // tm, d // tn),
        in_specs=[pl.BlockSpec((tm, rec), lambda i, j: (i, 0)),
                  pl.BlockSpec((tm, xa), lambda i, j: (i, 0)),
                  pl.BlockSpec((rec + xa, tn), lambda i, j: (0, j))],
        out_specs=pl.BlockSpec((tm, tn), lambda i, j: (i, j)),
        compiler_params=_params(("parallel", "arbitrary")),
        name="out_proj",
    )(ymix, mmix, w)


def _resid_kernel(o_ref, h_ref, pg_ref, ng_ref, ho_ref, hn_ref):
    o = o_ref[...].astype(F32)
    h = h_ref[...] + o * _rms_scale(o) * pg_ref[...]
    ho_ref[...] = h
    hn_ref[...] = (h * _rms_scale(h) * ng_ref[...]).astype(hn_ref.dtype)


def _resid_norm(o, h, post_g, next_g):
    t, d = h.shape
    tm = _tile(t, 256)
    row = pl.BlockSpec((tm, d), lambda i: (i, 0))
    vec = pl.BlockSpec((1, d), lambda i: (0, 0))
    return pl.pallas_call(
        _resid_kernel,
        out_shape=(jax.ShapeDtypeStruct((t, d), F32), jax.ShapeDtypeStruct((t, d), BF16)),
        grid=(t // tm,),
        in_specs=[row, row, vec, vec],
        out_specs=(row, row),
        compiler_params=_params(("parallel",)),
        name="resid_norm",
    )(o, h, post_g.reshape(1, d), next_g.reshape(1, d))


def _final_kernel(o_ref, h_ref, pg_ref, out_ref):
    pg = pg_ref[...]
    for tl in range(TL):
        o = o_ref[tl].astype(F32)
        out_ref[:, tl, :] = h_ref[tl] + o * _rms_scale(o) * pg


def _final_resid(o, h, post_g, batch, seq):
    d = h.shape[1]
    ch = seq // (TL * CL)
    in_spec = pl.BlockSpec((None, TL, None, ch, d), lambda i, j: (i, 0, j, 0, 0))
    out = pl.pallas_call(
        _final_kernel,
        out_shape=jax.ShapeDtypeStruct((batch, ch, CL, TL, d), F32),
        grid=(batch, CL),
        in_specs=[in_spec, in_spec, pl.BlockSpec((1, d), lambda i, j: (0, 0))],
        out_specs=pl.BlockSpec((None, ch, None, TL, d), lambda i, j: (i, 0, j, 0, 0)),
        compiler_params=_params(("parallel", "parallel")),
        name="final_resid",
    )(o.reshape(batch, TL, CL, ch, d), h.reshape(batch, TL, CL, ch, d), post_g.reshape(1, d))
    return out.reshape(batch, seq, d)


def _kv_kernel(mem_ref, g_ref, w_ref, o_ref, xn_ref):
    @pl.when(pl.program_id(2) == 0)
    def _():
        x = mem_ref[...]
        xn_ref[...] = (x * _rms_scale(x) * g_ref[...]).astype(xn_ref.dtype)

    o_ref[...] = jnp.dot(xn_ref[...], w_ref[...], preferred_element_type=F32).astype(o_ref.dtype)


def _kv_proj(mem, mem_norm, w_kv):
    b, n, d = mem.shape
    depth, _, n2 = w_kv.shape
    tn = _tile(n2, 512)
    return pl.pallas_call(
        _kv_kernel,
        out_shape=jax.ShapeDtypeStruct((depth, b, n, n2), BF16),
        grid=(b, depth, n2 // tn),
        in_specs=[pl.BlockSpec((None, n, d), lambda i, l, j: (i, 0, 0)),
                  pl.BlockSpec((None, 1, d), lambda i, l, j: (l, 0, 0)),
                  pl.BlockSpec((None, d, tn), lambda i, l, j: (l, 0, j))],
        out_specs=pl.BlockSpec((None, None, n, tn), lambda i, l, j: (l, i, 0, j)),
        scratch_shapes=[pltpu.VMEM((n, d), BF16)],
        compiler_params=_params(("parallel", "arbitrary", "arbitrary")),
        name="kv_proj",
    )(mem, mem_norm.reshape(depth, 1, d), w_kv)


def _attn_kernel(q_ref, qg_ref, kv_ref, o_ref, *, xa, hd):
    scale = hd ** -0.5
    for h in range(XA_HEADS):
        q = q_ref[:, h * hd:(h + 1) * hd]
        k = kv_ref[:, h * hd:(h + 1) * hd]
        v = kv_ref[:, xa + h * hd:xa + (h + 1) * hd]
        s = lax.dot_general(q, k, (((1,), (1,)), ((), ())), preferred_element_type=F32) * scale
        p = jnp.exp(s - jnp.max(s, axis=-1, keepdims=True))
        denom = jnp.sum(p, axis=-1, keepdims=True)
        o = jnp.dot(p.astype(BF16), v, preferred_element_type=F32) / denom
        qg = qg_ref[:, h * hd:(h + 1) * hd].astype(F32)
        o_ref[:, h * hd:(h + 1) * hd] = (o * _silu(qg)).astype(o_ref.dtype)


def _mem_attention(proj, kv_l, layer, batch, rec, xa):
    t = proj.shape[0]
    rows = t // batch
    n_mem = kv_l.shape[2]
    tm = _tile(rows, 1024)
    per_b = rows // tm
    qcol = (2 * rec) // xa
    hd = xa // XA_HEADS
    return pl.pallas_call(
        functools.partial(_attn_kernel, xa=xa, hd=hd),
        out_shape=jax.ShapeDtypeStruct((t, xa), BF16),
        grid=(batch, per_b),
        in_specs=[pl.BlockSpec((tm, xa), lambda b, i: (b * per_b + i, qcol)),
                  pl.BlockSpec((tm, xa), lambda b, i: (b * per_b + i, qcol + 1)),
                  pl.BlockSpec((None, None, n_mem, 2 * xa), lambda b, i: (layer, b, 0, 0))],
        out_specs=pl.BlockSpec((tm, xa), lambda b, i: (b * per_b + i, 0)),
        compiler_params=_params(("parallel", "arbitrary")),
        name="mem_attn",
    )(proj, proj, kv_l)


def _cmul(s, a1, a2, half):
    return s * a1 + pltpu.roll(s, half, axis=1) * a2


def _shift_rows(x, k, period, fill):
    rows = lax.broadcasted_iota(jnp.int32, x.shape, 0)
    return jnp.where((rows & (period - 1)) < k, fill, pltpu.roll(x, k, axis=0))


def _s5_kernel(u_ref, kw_ref, be_ref, cp_ref, lt_ref, d_ref, o_ref, m_ref, t1_ref, ug_ref, z_ref, l_ref, s_ref,
               t2_ref, *, c, ch, gs, gch, p):
    p2 = 2 * p

    @pl.when(pl.program_id(1) == 0)
    def _():
        lanes = lax.broadcasted_iota(jnp.int32, (gch, TL * gch), 1)
        for g in range(gs):
            kw = kw_ref[g]
            m_ref[g, 0:gch, :] = kw.astype(m_ref.dtype)
            for tin in range(1, TL):
                blk = jnp.where(lanes < tin * gch, 0.0, pltpu.roll(kw, tin * gch, axis=1))
                m_ref[g, tin * gch:(tin + 1) * gch, :] = blk.astype(m_ref.dtype)

    for tl in range(TL):
        t1_ref[tl] = u_ref[tl * c:(tl + 1) * c, :].astype(F32).T
    for g in range(gs):
        vt = t1_ref[:, g * gch:(g + 1) * gch, :].reshape(TL * gch, c)
        ug = vt.T.astype(BF16)
        ug_ref[g] = ug
        z_ref[g] = jnp.dot(ug, be_ref[g], preferred_element_type=F32)

    def ratio(row):
        a1 = jnp.broadcast_to(lt_ref[:, row:row + 1, :], (gs, ch, p2)).reshape(gs * ch, p2)
        a2 = jnp.broadcast_to(lt_ref[:, row + 1:row + 2, :], (gs, ch, p2)).reshape(gs * ch, p2)
        return a1, a2

    def zslab(ref, cl):
        return ref[:, cl * ch:(cl + 1) * ch, :].reshape(gs * ch, p2)

    a1, a2 = ratio(0)
    lc = zslab(z_ref, 0)
    l_ref[:, 0:ch, :] = lc.reshape(gs, ch, p2)
    for cl in range(1, CL):
        lc = _cmul(lc, a1, a2, p) + zslab(z_ref, cl)
        l_ref[:, cl * ch:(cl + 1) * ch, :] = lc.reshape(gs, ch, p2)
    tot = lc
    k, row = 1, 2 + 2 * CL
    while k < ch:
        b1, b2 = ratio(row)
        tot = tot + _cmul(_shift_rows(tot, k, ch, 0.0), b1, b2, p)
        k, row = 2 * k, row + 2
    cy = _shift_rows(tot, 1, ch, 0.0)
    s_ref[:, 0:ch, :] = cy.reshape(gs, ch, p2).astype(s_ref.dtype)
    for cl in range(1, CL):
        b1, b2 = ratio(2 + 2 * cl)
        s_in = zslab(l_ref, cl - 1) + _cmul(cy, b1, b2, p)
        s_ref[:, cl * ch:(cl + 1) * ch, :] = s_in.reshape(gs, ch, p2).astype(s_ref.dtype)

    for g in range(gs):
        ug = ug_ref[g]
        y = jnp.dot(ug, m_ref[g], preferred_element_type=F32)
        y = y + jnp.dot(s_ref[g], cp_ref[g], preferred_element_type=F32)
        y = y + d_ref[g] * ug.astype(F32)
        t2_ref[:, g * gch:(g + 1) * gch, :] = _gelu_tanh(y).T.reshape(TL, gch, c)
    for tl in range(TL):
        o_ref[tl * c:(tl + 1) * c, :] = t2_ref[tl].T.astype(o_ref.dtype)


def _s5_tables(lam_re, lam_im, log_step, b_re, b_im, c_re, c_im, d_skip, ch):
    g, p, c = b_re.shape
    hi = lax.Precision.HIGHEST
    lr = jnp.minimum(lam_re.astype(F32), -1e-4)
    li = lam_im.astype(F32)
    dt = jnp.exp(log_step.astype(F32))[:, None]

    def lam_pow(k):
        k = jnp.asarray(k, F32)[..., None, None]
        mag = jnp.exp(lr * dt * k)
        return mag * jnp.cos(li * dt * k), mag * jnp.sin(li * dt * k)

    ab_re, ab_im = lam_pow(1.0)
    den = lr * lr + li * li
    nr, ni = ab_re - 1.0, ab_im
    coef_re = (nr * lr + ni * li) / den
    coef_im = (ni * lr - nr * li) / den
    br, bi = b_re.astype(F32), b_im.astype(F32)
    bb_re = coef_re[..., None] * br - coef_im[..., None] * bi
    bb_im = coef_re[..., None] * bi + coef_im[..., None] * br
    cr, ci = c_re.astype(F32), c_im.astype(F32)

    steps = jnp.arange(TL + 1)
    pr, pi = lam_pow(steps)
    cl_re = cr[None] * pr[:, :, None, :] - ci[None] * pi[:, :, None, :]
    cl_im = cr[None] * pi[:, :, None, :] + ci[None] * pr[:, :, None, :]
    klag = (jnp.einsum('kgop,gpi->gkoi', cl_re[:TL], bb_re, precision=hi)
            - jnp.einsum('kgop,gpi->gkoi', cl_im[:TL], bb_im, precision=hi))
    kw = klag.transpose(0, 3, 1, 2).reshape(g, c, TL * c)

    qr, qi = pr[TL - 1 - jnp.arange(TL)], pi[TL - 1 - jnp.arange(TL)]
    be_re = qr[..., None] * bb_re[None] - qi[..., None] * bb_im[None]
    be_im = qr[..., None] * bb_im[None] + qi[..., None] * bb_re[None]
    bend = jnp.concatenate([be_re, be_im], axis=2)
    bend = bend.transpose(1, 0, 3, 2).reshape(g, TL * c, 2 * p)

    cp = jnp.concatenate([cl_re[1:], -cl_im[1:]], axis=3)
    cpow = cp.transpose(1, 3, 0, 2).reshape(g, 2 * p, TL * c)

    ks = [float(TL)] + [float(TL * cl) for cl in range(CL)]
    k = 1
    while k < ch:
        ks.append(float(TL * CL * k))
        k *= 2
    wr, wi = lam_pow(jnp.array(ks, F32))
    rows = jnp.stack([jnp.concatenate([wr, wr], axis=-1),
                      jnp.concatenate([-wi, wi], axis=-1)], axis=1)
    ltab = rows.reshape(2 * len(ks), g, 2 * p).transpose(1, 0, 2)
    pad = (-ltab.shape[1]) % 8
    ltab = jnp.pad(ltab, ((0, 0), (0, pad), (0, 0)))
    dtile = jnp.tile(d_skip.astype(F32).reshape(g, 1, c), (1, 1, TL))
    return kw, bend.astype(BF16), cpow.astype(BF16), ltab, dtile


def _s5_ssm(proj, tables, layer, batch, rec):
    kw, bend, cpow, ltab, dtile = tables
    t = proj.shape[0]
    rows = t // batch
    c = rows // TL
    _, g, w, p2 = bend.shape
    gch = w // TL
    gs = LANE_SLAB // gch
    nt = ltab.shape[2]
    wspec = lambda *shape: pl.BlockSpec((None, gs) + shape, lambda j, b: (layer, j, 0, 0))
    return pl.pallas_call(
        functools.partial(_s5_kernel, c=c, ch=c // CL, gs=gs, gch=gch, p=p2 // 2),
        out_shape=jax.ShapeDtypeStruct((t, rec), BF16),
        grid=(g // gs, batch),
        in_specs=[pl.BlockSpec((rows, LANE_SLAB), lambda j, b: (b, j)),
                  wspec(gch, w), wspec(w, p2), wspec(p2, w), wspec(nt, p2), wspec(1, w)],
        out_specs=pl.BlockSpec((rows, LANE_SLAB), lambda j, b: (b, j)),
        scratch_shapes=[pltpu.VMEM((gs, w, w), BF16),
                        pltpu.VMEM((TL, LANE_SLAB, c), F32),
                        pltpu.VMEM((gs, c, w), BF16),
                        pltpu.VMEM((gs, c, p2), F32),
                        pltpu.VMEM((gs, c, p2), F32),
                        pltpu.VMEM((gs, c, p2), BF16),
                        pltpu.VMEM((TL, LANE_SLAB, c), F32)],
        compiler_params=_params(("parallel", "arbitrary")),
        name="s5_ssm",
    )(proj, kw, bend, cpow, ltab, dtile)


def _glu_kernel(g_ref, w_ref, b_ref, gn_ref, gate_ref, o_ref):
    z = jnp.dot(g_ref[...], w_ref[...], preferred_element_type=F32) + b_ref[...]
    gn = gn_ref[...].astype(F32)
    gate = gate_ref[...].astype(F32)
    o_ref[...] = (gn * jax.nn.sigmoid(z) * _silu(gate)).astype(o_ref.dtype)


def _glu_gate(gact, w_all, b_glu, layer, proj):
    t, rec = gact.shape
    tm, tn = _tile(t, 1024), _tile(rec, 1024)
    goff = rec // tn
    return pl.pallas_call(
        _glu_kernel,
        out_shape=jax.ShapeDtypeStruct((t, rec), BF16),
        grid=(t // tm, rec // tn),
        in_specs=[pl.BlockSpec((tm, rec), lambda i, j: (i, 0)),
                  pl.BlockSpec((None, rec, tn), lambda i, j: (layer, 0, j)),
                  pl.BlockSpec((1, tn), lambda i, j: (0, j)),
                  pl.BlockSpec((tm, tn), lambda i, j: (i, j)),
                  pl.BlockSpec((tm, tn), lambda i, j: (i, goff + j))],
        out_specs=pl.BlockSpec((tm, tn), lambda i, j: (i, j)),
        compiler_params=_params(("parallel", "arbitrary")),
        name="s5_glu",
    )(gact, w_all, b_glu.reshape(1, rec), gact, proj)


def _prev_chunk(x, ch):
    rows = lax.broadcasted_iota(jnp.int32, x.shape, 0)
    wrapped = jnp.where(rows == 0, 0.0, pltpu.roll(x, ch + 1, axis=0))
    return jnp.where(rows < ch, wrapped, pltpu.roll(x, ch, axis=0))


def _lru_kernel(u_ref, gate_ref, cw_ref, cb_ref, wa_ref, ba_ref, wx_ref, bx_ref, lam_ref, o_ref,
                hs_ref, ps_ref, uf_ref, *, c, ch):
    neg_lam = -lam_ref[...]
    sp = jnp.maximum(neg_lam, 0.0) + jnp.log1p(jnp.exp(-jnp.abs(neg_lam)))
    rate = sp * (-LRU_C / math.log(2.0))
    cw = cw_ref[...]
    cb = cb_ref[...]
    wa, wx = wa_ref[...], wx_ref[...]
    ba, bx = ba_ref[...], bx_ref[...]
    conv_w = cw.shape[0]
    uf_ref[...] = u_ref[...].astype(F32)

    def slab(tl):
        return uf_ref[tl * c:(tl + 1) * c, :]

    h = None
    pr = None
    for tl in range(TL):
        xc = cb + cw[conv_w - 1:conv_w] * slab(tl)
        for j in range(1, conv_w):
            wj = cw[conv_w - 1 - j:conv_w - j]
            if tl - j >= 0:
                xc = xc + wj * slab(tl - j)
            else:
                xc = xc + wj * _prev_chunk(slab(tl - j + TL), ch)
        xcb = xc.astype(BF16)
        r = jax.nn.sigmoid(jnp.dot(xcb, wa, preferred_element_type=F32) + ba)
        ig = jax.nn.sigmoid(jnp.dot(xcb, wx, preferred_element_type=F32) + bx)
        a = jnp.exp2(r * rate)
        mult = jnp.sqrt(1.0 - a * a)
        if tl == 0:
            rows = lax.broadcasted_iota(jnp.int32, mult.shape, 0)
            mult = jnp.where(rows == 0, 1.0, mult)
        bt = mult * (ig * xc)
        if tl == 0:
            h, pr = bt, a
        else:
            h, pr = a * h + bt, a * pr
        hs_ref[tl * c:(tl + 1) * c, :] = h
        ps_ref[tl * c:(tl + 1) * c, :] = pr
    av, bv = pr[0:ch], h[0:ch]
    a_pre, b_pre = [av], [bv]
    for cl in range(1, CL):
        pc, hc = pr[cl * ch:(cl + 1) * ch], h[cl * ch:(cl + 1) * ch]
        av, bv = pc * av, pc * bv + hc
        a_pre.append(av)
        b_pre.append(bv)
    k = 1
    while k < ch:
        bv = bv + av * _shift_rows(bv, k, ch, 0.0)
        av = av * _shift_rows(av, k, ch, 1.0)
        k *= 2
    cy = _shift_rows(bv, 1, ch, 0.0)
    carry = jnp.concatenate([cy] + [a_pre[cl - 1] * cy + b_pre[cl - 1] for cl in range(1, CL)], axis=0)
    for tl in range(TL):
        hf = hs_ref[tl * c:(tl + 1) * c, :] + ps_ref[tl * c:(tl + 1) * c, :] * carry
        gate = gate_ref[tl * c:(tl + 1) * c, :].astype(F32)
        o_ref[tl * c:(tl + 1) * c, :] = (hf * _silu(gate)).astype(o_ref.dtype)


def _rglru(proj, conv_w, conv_b, w_a_all, b_a, w_x_all, b_x, lam, layer, batch, rec):
    t = proj.shape[0]
    rows = t // batch
    _, nblk, blk, _ = w_a_all.shape
    cwid = conv_w.shape[0]
    goff = rec // blk
    c = rows // TL
    vec = lambda v: v.astype(F32).reshape(nblk, 1, blk)
    wspec = pl.BlockSpec((None, None, blk, blk), lambda b, j: (layer, j, 0, 0))
    return pl.pallas_call(
        functools.partial(_lru_kernel, c=c, ch=c // CL),
        out_shape=jax.ShapeDtypeStruct((t, rec), BF16),
        grid=(batch, nblk),
        in_specs=[pl.BlockSpec((rows, blk), lambda b, j: (b, j)),
                  pl.BlockSpec((rows, blk), lambda b, j: (b, goff + j)),
                  pl.BlockSpec((cwid, blk), lambda b, j: (0, j)),
                  pl.BlockSpec((None, 1, blk), lambda b, j: (j, 0, 0)),
                  wspec,
                  pl.BlockSpec((None, 1, blk), lambda b, j: (j, 0, 0)),
                  wspec,
                  pl.BlockSpec((None, 1, blk), lambda b, j: (j, 0, 0)),
                  pl.BlockSpec((None, 1, blk), lambda b, j: (j, 0, 0))],
        out_specs=pl.BlockSpec((rows, blk), lambda b, j: (b, j)),
        scratch_shapes=[pltpu.VMEM((rows, blk), F32)] * 3,
        compiler_params=_params(("parallel", "arbitrary")),
        name="rglru",
    )(proj, proj, conv_w.astype(F32), vec(conv_b), w_a_all, vec(b_a), w_x_all, vec(b_x), vec(lam))


def kernel(x, mem, w_in, w_kv, w_out, pre_norm, post_norm, mem_norm, s5_lam_re, s5_lam_im, s5_log_step, s5_b_re, s5_b_im, s5_c_re, s5_c_im, s5_d, s5_w_glu, s5_b_glu, lru_conv_w, lru_conv_b, lru_w_a, lru_b_a, lru_w_x, lru_b_x, lru_lam):
    batch, seq, d = x.shape
    depth = w_in.shape[0]
    xa = w_kv.shape[2] // 2
    rec = w_out.shape[1] - xa
    groups, _, gch = s5_b_re.shape[1:]
    assert seq % (TL * CL) == 0 and TL * gch == LANE_SLAB and rec == groups * gch
    ch = seq // (TL * CL)
    assert ch & (ch - 1) == 0, "row-shift masks assume a power-of-two super-chunk count"

    assert depth >= 2, "first and last layers use distinct residual kernels"

    w_in_cur = w_in[0].astype(BF16)
    tables = jax.vmap(functools.partial(_s5_tables, ch=ch))(
        s5_lam_re, s5_lam_im, s5_log_step, s5_b_re, s5_b_im, s5_c_re, s5_c_im, s5_d)
    stacked = [w_kv, s5_w_glu, lru_w_a, lru_w_x]
    flat = lambda w: w.reshape(1, -1, w.shape[-1])

    hn = _entry(x, pre_norm[0])
    h = None
    for i in range(depth):
        j = i // 2
        casts = [(w_out, i)] + ([(w_in, i + 1)] if i + 1 < depth else [])
        if i == 0:
            casts += [(flat(w), 0) for w in stacked]
        proj, cast_out = _in_proj(hn, w_in_cur, casts)
        w_out_cur = cast_out[0]
        if i + 1 < depth:
            w_in_cur = cast_out[1]
        if i == 0:
            w_kv_b, w_glu_b, w_a_b, w_x_b = [c.reshape(w.shape) for c, w in zip(cast_out[2:], stacked)]
            kv = _kv_proj(mem, mem_norm, w_kv_b)
        if i % 2 == 0:
            gact = _s5_ssm(proj, tables, j, batch, rec)
            ymix = _glu_gate(gact, w_glu_b, s5_b_glu[j].astype(F32), j, proj)
        else:
            ymix = _rglru(proj, lru_conv_w[j], lru_conv_b[j], w_a_b, lru_b_a[j],
                          w_x_b, lru_b_x[j], lru_lam[j], j, batch, rec)
        mmix = _mem_attention(proj, kv, i, batch, rec, xa)
        o = _out_proj(ymix, mmix, w_out_cur)
        if i == 0:
            h, hn = _first_resid_norm(o, x, post_norm[i], pre_norm[i + 1])
        elif i + 1 < depth:
            h, hn = _resid_norm(o, h, post_norm[i], pre_norm[i + 1])
        else:
            h = _final_resid(o, h, post_norm[i], batch, seq)
    return h
```

```python
import functools
import math

import jax
import jax.numpy as jnp
from jax import lax
from jax.experimental import pallas as pl
from jax.experimental.pallas import tpu as pltpu

F32 = jnp.float32
BF16 = jnp.bfloat16

EPS = 1e-6
XA_HEADS = 4
LRU_C = 8.0
TL = 16
CL = 16
LANE_SLAB = 256
VMEM_LIMIT_V7X = 56 * 1024 * 1024


def _tile(n, pref):
    if n <= pref:
        return n
    t = (pref // 128) * 128
    while n % t:
        t -= 128
    return t


def _params(semantics):
    return pltpu.CompilerParams(dimension_semantics=semantics, vmem_limit_bytes=VMEM_LIMIT_V7X)


def _silu(x):
    return x * jax.nn.sigmoid(x)


def _gelu_tanh(x):
    c = math.sqrt(2.0 / math.pi)
    return 0.5 * x * (1.0 + jnp.tanh(c * (x + 0.044715 * (x * x * x))))


def _rms_scale(x):
    return lax.rsqrt(jnp.mean(x * x, axis=-1, keepdims=True) + EPS)


def _time_spec(ch, d):
    return pl.BlockSpec((None, ch, None, TL, d), lambda i, j: (i, 0, j, 0, 0))


def _trunk_spec(ch, d):
    return pl.BlockSpec((None, TL, None, ch, d), lambda i, j: (i, 0, j, 0, 0))


def _entry_kernel(x_ref, g_ref, hn_ref, t_ref):
    g = g_ref[...]
    for ch in range(x_ref.shape[0]):
        t_ref[:, ch, :] = x_ref[ch]
    for tl in range(TL):
        x = t_ref[tl]
        hn_ref[tl] = (x * _rms_scale(x) * g).astype(hn_ref.dtype)


def _entry(x, g):
    b, seq, d = x.shape
    ch = seq // (TL * CL)
    hn = pl.pallas_call(
        _entry_kernel,
        out_shape=jax.ShapeDtypeStruct((b, TL, CL, ch, d), BF16),
        grid=(b, CL),
        in_specs=[_time_spec(ch, d), pl.BlockSpec((1, d), lambda i, j: (0, 0))],
        out_specs=_trunk_spec(ch, d),
        scratch_shapes=[pltpu.VMEM((TL, ch, d), F32)],
        compiler_params=_params(("parallel", "parallel")),
        name="entry_norm",
    )(x.reshape(b, ch, CL, TL, d), g.reshape(1, d))
    return hn.reshape(b * seq, d)


def _first_resid_kernel(o_ref, x_ref, pg_ref, ng_ref, ho_ref, hn_ref):
    pg, ng = pg_ref[...], ng_ref[...]
    for ch in range(x_ref.shape[0]):
        ho_ref[:, ch, :] = x_ref[ch]
    for tl in range(TL):
        o = o_ref[tl].astype(F32)
        h = ho_ref[tl] + o * _rms_scale(o) * pg
        ho_ref[tl] = h
        hn_ref[tl] = (h * _rms_scale(h) * ng).astype(hn_ref.dtype)


def _first_resid_norm(o, x, post_g, next_g):
    b, seq, d = x.shape
    ch = seq // (TL * CL)
    vec = pl.BlockSpec((1, d), lambda i, j: (0, 0))
    h, hn = pl.pallas_call(
        _first_resid_kernel,
        out_shape=(jax.ShapeDtypeStruct((b, TL, CL, ch, d), F32),
                   jax.ShapeDtypeStruct((b, TL, CL, ch, d), BF16)),
        grid=(b, CL),
        in_specs=[_trunk_spec(ch, d), _time_spec(ch, d), vec, vec],
        out_specs=(_trunk_spec(ch, d), _trunk_spec(ch, d)),
        compiler_params=_params(("parallel", "parallel")),
        name="first_resid",
    )(o.reshape(b, TL, CL, ch, d), x.reshape(b, ch, CL, TL, d), post_g.reshape(1, d), next_g.reshape(1, d))
    return h.reshape(b * seq, d), hn.reshape(b * seq, d)


def _in_proj_kernel(x_ref, w_ref, *refs):
    n_cast = (len(refs) - 1) // 2
    o_ref = refs[n_cast]
    o_ref[...] = jnp.dot(x_ref[...], w_ref[...], preferred_element_type=F32).astype(o_ref.dtype)
    for src, dst in zip(refs[:n_cast], refs[n_cast + 1:]):
        dst[...] = src[...].astype(dst.dtype)


def _in_proj(hn, w, casts):
    t, d = hn.shape
    n = w.shape[1]
    tm, tn = _tile(t, 1024), _tile(n, 1024)
    ni, nj = t // tm, n // tn
    steps = ni * nj
    in_specs = [pl.BlockSpec((tm, d), lambda i, j: (i, 0)),
                pl.BlockSpec((d, tn), lambda i, j: (0, j))]
    out_shape = [jax.ShapeDtypeStruct((t, n), BF16)]
    out_specs = [pl.BlockSpec((tm, tn), lambda i, j: (i, j))]
    operands = [hn, w]
    for src, layer in casts:
        _, r, c = src.shape
        strip = r // steps
        assert strip * steps == r and strip % 16 == 0, "bf16 strips must cover whole (16, 128) tiles"
        in_specs.append(pl.BlockSpec((None, strip, c), lambda i, j, layer=layer: (layer, i * nj + j, 0)))
        out_shape.append(jax.ShapeDtypeStruct((r, c), BF16))
        out_specs.append(pl.BlockSpec((strip, c), lambda i, j: (i * nj + j, 0)))
        operands.append(src)
    res = pl.pallas_call(
        _in_proj_kernel,
        out_shape=tuple(out_shape),
        grid=(ni, nj),
        in_specs=in_specs,
        out_specs=tuple(out_specs),
        compiler_params=_params(("parallel", "arbitrary")),
        name="in_proj",
    )(*operands)
    return res[0], list(res[1:])


def _out_matmul_kernel(y_ref, m_ref, w_ref, o_ref, *, rec):
    acc = jnp.dot(y_ref[...], w_ref[0:rec, :], preferred_element_type=F32)
    acc = acc + jnp.dot(m_ref[...], w_ref[rec:, :], preferred_element_type=F32)
    o_ref[...] = acc.astype(o_ref.dtype)


def _out_proj(ymix, mmix, w):
    t, rec = ymix.shape
    xa = mmix.shape[1]
    d = w.shape[1]
    tm, tn = _tile(t, 1024), _tile(d, 1024)
    return pl.pallas_call(
        functools.partial(_out_matmul_kernel, rec=rec),
        out_shape=jax.ShapeDtypeStruct((t, d), BF16),
        grid=(t // tm, d // tn),
        in_specs=[pl.BlockSpec((tm, rec), lambda i, j: (i, 0)),
                  pl.BlockSpec((tm, xa), lambda i, j: (i, 0)),
                  pl.BlockSpec((rec + xa, tn), lambda i, j: (0, j))],
        out_specs=pl.BlockSpec((tm, tn), lambda i, j: (i, j)),
        compiler_params=_params(("parallel", "arbitrary")),
        name="out_proj",
    )(ymix, mmix, w)


def _resid_kernel(o_ref, h_ref, pg_ref, ng_ref, ho_ref, hn_ref):
    o = o_ref[...].astype(F32)
    h = h_ref[...] + o * _rms_scale(o) * pg_ref[...]
    ho_ref[...] = h
    hn_ref[...] = (h * _rms_scale(h) * ng_ref[...]).astype(hn_ref.dtype)


def _resid_norm(o, h, post_g, next_g):
    t, d = h.shape
    tm = _tile(t, 256)
    row = pl.BlockSpec((tm, d), lambda i: (i, 0))
    vec = pl.BlockSpec((1, d), lambda i: (0, 0))
    return pl.pallas_call(
        _resid_kernel,
        out_shape=(jax.ShapeDtypeStruct((t, d), F32), jax.ShapeDtypeStruct((t, d), BF16)),
        grid=(t // tm,),
        in_specs=[row, row, vec, vec],
        out_specs=(row, row),
        compiler_params=_params(("parallel",)),
        name="resid_norm",
    )(o, h, post_g.reshape(1, d), next_g.reshape(1, d))


def _final_kernel(o_ref, h_ref, pg_ref, out_ref):
    pg = pg_ref[...]
    for tl in range(TL):
        o = o_ref[tl].astype(F32)
        out_ref[:, tl, :] = h_ref[tl] + o * _rms_scale(o) * pg


def _final_resid(o, h, post_g, batch, seq):
    d = h.shape[1]
    ch = seq // (TL * CL)
    in_spec = pl.BlockSpec((None, TL, None, ch, d), lambda i, j: (i, 0, j, 0, 0))
    out = pl.pallas_call(
        _final_kernel,
        out_shape=jax.ShapeDtypeStruct((batch, ch, CL, TL, d), F32),
        grid=(batch, CL),
        in_specs=[in_spec, in_spec, pl.BlockSpec((1, d), lambda i, j: (0, 0))],
        out_specs=pl.BlockSpec((None, ch, None, TL, d), lambda i, j: (i, 0, j, 0, 0)),
        compiler_params=_params(("parallel", "parallel")),
        name="final_resid",
    )(o.reshape(batch, TL, CL, ch, d), h.reshape(batch, TL, CL, ch, d), post_g.reshape(1, d))
    return out.reshape(batch, seq, d)


def _kv_kernel(mem_ref, g_ref, w_ref, o_ref, xn_ref):
    @pl.when(pl.program_id(2) == 0)
    def _():
        x = mem_ref[...]
        xn_ref[...] = (x * _rms_scale(x) * g_ref[...]).astype(xn_ref.dtype)

    o_ref[...] = jnp.dot(xn_ref[...], w_ref[...], preferred_element_type=F32).astype(o_ref.dtype)


def _kv_proj(mem, mem_norm, w_kv):
    b, n, d = mem.shape
    depth, _, n2 = w_kv.shape
    tn = _tile(n2, 512)
    return pl.pallas_call(
        _kv_kernel,
        out_shape=jax.ShapeDtypeStruct((depth, b, n, n2), BF16),
        grid=(b, depth, n2 // tn),
        in_specs=[pl.BlockSpec((None, n, d), lambda i, l, j: (i, 0, 0)),
                  pl.BlockSpec((None, 1, d), lambda i, l, j: (l, 0, 0)),
                  pl.BlockSpec((None, d, tn), lambda i, l, j: (l, 0, j))],
        out_specs=pl.BlockSpec((None, None, n, tn), lambda i, l, j: (l, i, 0, j)),
        scratch_shapes=[pltpu.VMEM((n, d), BF16)],
        compiler_params=_params(("parallel", "arbitrary", "arbitrary")),
        name="kv_proj",
    )(mem, mem_norm.reshape(depth, 1, d), w_kv)


def _attn_kernel(q_ref, qg_ref, kv_ref, o_ref, *, xa, hd):
    scale = hd ** -0.5
    for h in range(XA_HEADS):
        q = q_ref[:, h * hd:(h + 1) * hd]
        k = kv_ref[:, h * hd:(h + 1) * hd]
        v = kv_ref[:, xa + h * hd:xa + (h + 1) * hd]
        s = lax.dot_general(q, k, (((1,), (1,)), ((), ())), preferred_element_type=F32) * scale
        p = jnp.exp(s - jnp.max(s, axis=-1, keepdims=True))
        denom = jnp.sum(p, axis=-1, keepdims=True)
        o = jnp.dot(p.astype(BF16), v, preferred_element_type=F32) / denom
        qg = qg_ref[:, h * hd:(h + 1) * hd].astype(F32)
        o_ref[:, h * hd:(h + 1) * hd] = (o * _silu(qg)).astype(o_ref.dtype)


def _mem_attention(proj, kv_l, layer, batch, rec, xa):
    t = proj.shape[0]
    rows = t // batch
    n_mem = kv_l.shape[2]
    tm = _tile(rows, 1024)
    per_b = rows // tm
    qcol = (2 * rec) // xa
    hd = xa // XA_HEADS
    return pl.pallas_call(
        functools.partial(_attn_kernel, xa=xa, hd=hd),
        out_shape=jax.ShapeDtypeStruct((t, xa), BF16),
        grid=(batch, per_b),
        in_specs=[pl.BlockSpec((tm, xa), lambda b, i: (b * per_b + i, qcol)),
                  pl.BlockSpec((tm, xa), lambda b, i: (b * per_b + i, qcol + 1)),
                  pl.BlockSpec((None, None, n_mem, 2 * xa), lambda b, i: (layer, b, 0, 0))],
        out_specs=pl.BlockSpec((tm, xa), lambda b, i: (b * per_b + i, 0)),
        compiler_params=_params(("parallel", "arbitrary")),
        name="mem_attn",
    )(proj, proj, kv_l)


def _cmul(s, a1, a2, half):
    return s * a1 + pltpu.roll(s, half, axis=1) * a2


def _shift_rows(x, k, period, fill):
    rows = lax.broadcasted_iota(jnp.int32, x.shape, 0)
    return jnp.where((rows & (period - 1)) < k, fill, pltpu.roll(x, k, axis=0))


def _s5_kernel(u_ref, kw_ref, be_ref, cp_ref, lt_ref, d_ref, o_ref, m_ref, t1_ref, ug_ref, z_ref, l_ref, s_ref,
               t2_ref, *, c, ch, gs, gch, p):
    p2 = 2 * p

    @pl.when(pl.program_id(1) == 0)
    def _():
        lanes = lax.broadcasted_iota(jnp.int32, (gch, TL * gch), 1)
        for g in range(gs):
            kw = kw_ref[g]
            m_ref[g, 0:gch, :] = kw.astype(m_ref.dtype)
            for tin in range(1, TL):
                blk = jnp.where(lanes < tin * gch, 0.0, pltpu.roll(kw, tin * gch, axis=1))
                m_ref[g, tin * gch:(tin + 1) * gch, :] = blk.astype(m_ref.dtype)

    for tl in range(TL):
        t1_ref[tl] = u_ref[tl * c:(tl + 1) * c, :].astype(F32).T
    for g in range(gs):
        vt = t1_ref[:, g * gch:(g + 1) * gch, :].reshape(TL * gch, c)
        ug = vt.T.astype(BF16)
        ug_ref[g] = ug
        z_ref[g] = jnp.dot(ug, be_ref[g], preferred_element_type=F32)

    def ratio(row):
        a1 = jnp.broadcast_to(lt_ref[:, row:row + 1, :], (gs, ch, p2)).reshape(gs * ch, p2)
        a2 = jnp.broadcast_to(lt_ref[:, row + 1:row + 2, :], (gs, ch, p2)).reshape(gs * ch, p2)
        return a1, a2

    def zslab(ref, cl):
        return ref[:, cl * ch:(cl + 1) * ch, :].reshape(gs * ch, p2)

    a1, a2 = ratio(0)
    lc = zslab(z_ref, 0)
    l_ref[:, 0:ch, :] = lc.reshape(gs, ch, p2)
    for cl in range(1, CL):
        lc = _cmul(lc, a1, a2, p) + zslab(z_ref, cl)
        l_ref[:, cl * ch:(cl + 1) * ch, :] = lc.reshape(gs, ch, p2)
    tot = lc
    k, row = 1, 2 + 2 * CL
    while k < ch:
        b1, b2 = ratio(row)
        tot = tot + _cmul(_shift_rows(tot, k, ch, 0.0), b1, b2, p)
        k, row = 2 * k, row + 2
    cy = _shift_rows(tot, 1, ch, 0.0)
    s_ref[:, 0:ch, :] = cy.reshape(gs, ch, p2).astype(s_ref.dtype)
    for cl in range(1, CL):
        b1, b2 = ratio(2 + 2 * cl)
        s_in = zslab(l_ref, cl - 1) + _cmul(cy, b1, b2, p)
        s_ref[:, cl * ch:(cl + 1) * ch, :] = s_in.reshape(gs, ch, p2).astype(s_ref.dtype)

    for g in range(gs):
        ug = ug_ref[g]
        y = jnp.dot(ug, m_ref[g], preferred_element_type=F32)
        y = y + jnp.dot(s_ref[g], cp_ref[g], preferred_element_type=F32)
        y = y + d_ref[g] * ug.astype(F32)
        t2_ref[:, g * gch:(g + 1) * gch, :] = _gelu_tanh(y).T.reshape(TL, gch, c)
    for tl in range(TL):
        o_ref[tl * c:(tl + 1) * c, :] = t2_ref[tl].T.astype(o_ref.dtype)


def _s5_tables(lam_re, lam_im, log_step, b_re, b_im, c_re, c_im, d_skip, ch):
    g, p, c = b_re.shape
    hi = lax.Precision.HIGHEST
    lr = jnp.minimum(lam_re.astype(F32), -1e-4)
    li = lam_im.astype(F32)
    dt = jnp.exp(log_step.astype(F32))[:, None]

    def lam_pow(k):
        k = jnp.asarray(k, F32)[..., None, None]
        mag = jnp.exp(lr * dt * k)
        return mag * jnp.cos(li * dt * k), mag * jnp.sin(li * dt * k)

    ab_re, ab_im = lam_pow(1.0)
    den = lr * lr + li * li
    nr, ni = ab_re - 1.0, ab_im
    coef_re = (nr * lr + ni * li) / den
    coef_im = (ni * lr - nr * li) / den
    br, bi = b_re.astype(F32), b_im.astype(F32)
    bb_re = coef_re[..., None] * br - coef_im[..., None] * bi
    bb_im = coef_re[..., None] * bi + coef_im[..., None] * br
    cr, ci = c_re.astype(F32), c_im.astype(F32)

    steps = jnp.arange(TL + 1)
    pr, pi = lax.optimization_barrier(lam_pow(steps))
    cl_re = cr[None] * pr[:, :, None, :] - ci[None] * pi[:, :, None, :]
    cl_im = cr[None] * pi[:, :, None, :] + ci[None] * pr[:, :, None, :]
    cl_re, cl_im, bb_re, bb_im = lax.optimization_barrier((cl_re, cl_im, bb_re, bb_im))
    klag = (jnp.einsum('kgop,gpi->gkoi', cl_re[:TL], bb_re, precision=hi)
            - jnp.einsum('kgop,gpi->gkoi', cl_im[:TL], bb_im, precision=hi))
    kw = klag.transpose(0, 3, 1, 2).reshape(g, c, TL * c)

    qr, qi = pr[TL - 1 - jnp.arange(TL)], pi[TL - 1 - jnp.arange(TL)]
    be_re = qr[..., None] * bb_re[None] - qi[..., None] * bb_im[None]
    be_im = qr[..., None] * bb_im[None] + qi[..., None] * bb_re[None]
    bend = jnp.concatenate([be_re, be_im], axis=2)
    bend = bend.transpose(1, 0, 3, 2).reshape(g, TL * c, 2 * p)

    cp = jnp.concatenate([cl_re[1:], -cl_im[1:]], axis=3)
    cpow = cp.transpose(1, 3, 0, 2).reshape(g, 2 * p, TL * c)

    ks = [float(TL)] + [float(TL * cl) for cl in range(CL)]
    k = 1
    while k < ch:
        ks.append(float(TL * CL * k))
        k *= 2
    wr, wi = lam_pow(jnp.array(ks, F32))
    rows = jnp.stack([jnp.concatenate([wr, wr], axis=-1),
                      jnp.concatenate([-wi, wi], axis=-1)], axis=1)
    ltab = rows.reshape(2 * len(ks), g, 2 * p).transpose(1, 0, 2)
    pad = (-ltab.shape[1]) % 8
    ltab = jnp.pad(ltab, ((0, 0), (0, pad), (0, 0)))
    dtile = jnp.tile(d_skip.astype(F32).reshape(g, 1, c), (1, 1, TL))
    return kw, bend.astype(BF16), cpow.astype(BF16), ltab, dtile


def _s5_ssm(proj, tables, layer, batch, rec):
    kw, bend, cpow, ltab, dtile = tables
    t = proj.shape[0]
    rows = t // batch
    c = rows // TL
    _, g, w, p2 = bend.shape
    gch = w // TL
    gs = LANE_SLAB // gch
    nt = ltab.shape[2]
    wspec = lambda *shape: pl.BlockSpec((None, gs) + shape, lambda j, b: (layer, j, 0, 0))
    return pl.pallas_call(
        functools.partial(_s5_kernel, c=c, ch=c // CL, gs=gs, gch=gch, p=p2 // 2),
        out_shape=jax.ShapeDtypeStruct((t, rec), BF16),
        grid=(g // gs, batch),
        in_specs=[pl.BlockSpec((rows, LANE_SLAB), lambda j, b: (b, j)),
                  wspec(gch, w), wspec(w, p2), wspec(p2, w), wspec(nt, p2), wspec(1, w)],
        out_specs=pl.BlockSpec((rows, LANE_SLAB), lambda j, b: (b, j)),
        scratch_shapes=[pltpu.VMEM((gs, w, w), BF16),
                        pltpu.VMEM((TL, LANE_SLAB, c), F32),
                        pltpu.VMEM((gs, c, w), BF16),
                        pltpu.VMEM((gs, c, p2), F32),
                        pltpu.VMEM((gs, c, p2), F32),
                        pltpu.VMEM((gs, c, p2), BF16),
                        pltpu.VMEM((TL, LANE_SLAB, c), F32)],
        compiler_params=_params(("parallel", "arbitrary")),
        name="s5_ssm",
    )(proj, kw, bend, cpow, ltab, dtile)


def _glu_kernel(g_ref, w_ref, b_ref, gn_ref, gate_ref, o_ref):
    z = jnp.dot(g_ref[...], w_ref[...], preferred_element_type=F32) + b_ref[...]
    gn = gn_ref[...].astype(F32)
    gate = gate_ref[...].astype(F32)
    o_ref[...] = (gn * jax.nn.sigmoid(z) * _silu(gate)).astype(o_ref.dtype)


def _glu_gate(gact, w_all, b_glu, layer, proj):
    t, rec = gact.shape
    tm, tn = _tile(t, 1024), _tile(rec, 1024)
    goff = rec // tn
    return pl.pallas_call(
        _glu_kernel,
        out_shape=jax.ShapeDtypeStruct((t, rec), BF16),
        grid=(t // tm, rec // tn),
        in_specs=[pl.BlockSpec((tm, rec), lambda i, j: (i, 0)),
                  pl.BlockSpec((None, rec, tn), lambda i, j: (layer, 0, j)),
                  pl.BlockSpec((1, tn), lambda i, j: (0, j)),
                  pl.BlockSpec((tm, tn), lambda i, j: (i, j)),
                  pl.BlockSpec((tm, tn), lambda i, j: (i, goff + j))],
        out_specs=pl.BlockSpec((tm, tn), lambda i, j: (i, j)),
        compiler_params=_params(("parallel", "arbitrary")),
        name="s5_glu",
    )(gact, w_all, b_glu.reshape(1, rec), gact, proj)


def _prev_chunk(x, ch):
    rows = lax.broadcasted_iota(jnp.int32, x.shape, 0)
    wrapped = jnp.where(rows == 0, 0.0, pltpu.roll(x, ch + 1, axis=0))
    return jnp.where(rows < ch, wrapped, pltpu.roll(x, ch, axis=0))


def _lru_kernel(u_ref, gate_ref, cw_ref, cb_ref, wa_ref, ba_ref, wx_ref, bx_ref, lam_ref, o_ref,
                hs_ref, ps_ref, uf_ref, *, c, ch):
    neg_lam = -lam_ref[...]
    sp = jnp.maximum(neg_lam, 0.0) + jnp.log1p(jnp.exp(-jnp.abs(neg_lam)))
    rate = sp * (-LRU_C / math.log(2.0))
    cw = cw_ref[...]
    cb = cb_ref[...]
    wa, wx = wa_ref[...], wx_ref[...]
    ba, bx = ba_ref[...], bx_ref[...]
    conv_w = cw.shape[0]
    uf_ref[...] = u_ref[...].astype(F32)

    def slab(tl):
        return uf_ref[tl * c:(tl + 1) * c, :]

    h = None
    pr = None
    for tl in range(TL):
        xc = cb + cw[conv_w - 1:conv_w] * slab(tl)
        for j in range(1, conv_w):
            wj = cw[conv_w - 1 - j:conv_w - j]
            if tl - j >= 0:
                xc = xc + wj * slab(tl - j)
            else:
                xc = xc + wj * _prev_chunk(slab(tl - j + TL), ch)
        xcb = xc.astype(BF16)
        r = jax.nn.sigmoid(jnp.dot(xcb, wa, preferred_element_type=F32) + ba)
        ig = jax.nn.sigmoid(jnp.dot(xcb, wx, preferred_element_type=F32) + bx)
        a = jnp.exp2(r * rate)
        mult = jnp.sqrt(1.0 - a * a)
        if tl == 0:
            rows = lax.broadcasted_iota(jnp.int32, mult.shape, 0)
            mult = jnp.where(rows == 0, 1.0, mult)
        bt = mult * (ig * xc)
        if tl == 0:
            h, pr = bt, a
        else:
            h, pr = a * h + bt, a * pr
        hs_ref[tl * c:(tl + 1) * c, :] = h
        ps_ref[tl * c:(tl + 1) * c, :] = pr
    av, bv = pr[0:ch], h[0:ch]
    a_pre, b_pre = [av], [bv]
    for cl in range(1, CL):
        pc, hc = pr[cl * ch:(cl + 1) * ch], h[cl * ch:(cl + 1) * ch]
        av, bv = pc * av, pc * bv + hc
        a_pre.append(av)
        b_pre.append(bv)
    k = 1
    while k < ch:
        bv = bv + av * _shift_rows(bv, k, ch, 0.0)
        av = av * _shift_rows(av, k, ch, 1.0)
        k *= 2
    cy = _shift_rows(bv, 1, ch, 0.0)
    carry = jnp.concatenate([cy] + [a_pre[cl - 1] * cy + b_pre[cl - 1] for cl in range(1, CL)], axis=0)
    for tl in range(TL):
        hf = hs_ref[tl * c:(tl + 1) * c, :] + ps_ref[tl * c:(tl + 1) * c, :] * carry
        gate = gate_ref[tl * c:(tl + 1) * c, :].astype(F32)
        o_ref[tl * c:(tl + 1) * c, :] = (hf * _silu(gate)).astype(o_ref.dtype)


def _rglru(proj, conv_w, conv_b, w_a_all, b_a, w_x_all, b_x, lam, layer, batch, rec):
    t = proj.shape[0]
    rows = t // batch
    _, nblk, blk, _ = w_a_all.shape
    cwid = conv_w.shape[0]
    goff = rec // blk
    c = rows // TL
    vec = lambda v: v.astype(F32).reshape(nblk, 1, blk)
    wspec = pl.BlockSpec((None, None, blk, blk), lambda b, j: (layer, j, 0, 0))
    return pl.pallas_call(
        functools.partial(_lru_kernel, c=c, ch=c // CL),
        out_shape=jax.ShapeDtypeStruct((t, rec), BF16),
        grid=(batch, nblk),
        in_specs=[pl.BlockSpec((rows, blk), lambda b, j: (b, j)),
                  pl.BlockSpec((rows, blk), lambda b, j: (b, goff + j)),
                  pl.BlockSpec((cwid, blk), lambda b, j: (0, j)),
                  pl.BlockSpec((None, 1, blk), lambda b, j: (j, 0, 0)),
                  wspec,
                  pl.BlockSpec((None, 1, blk), lambda b, j: (j, 0, 0)),
                  wspec,
                  pl.BlockSpec((None, 1, blk), lambda b, j: (j, 0, 0)),
                  pl.BlockSpec((None, 1, blk), lambda b, j: (j, 0, 0))],
        out_specs=pl.BlockSpec((rows, blk), lambda b, j: (b, j)),
        scratch_shapes=[pltpu.VMEM((rows, blk), F32)] * 3,
        compiler_params=_params(("parallel", "arbitrary")),
        name="rglru",
    )(proj, proj, conv_w.astype(F32), vec(conv_b), w_a_all, vec(b_a), w_x_all, vec(b_x), vec(lam))


def kernel(x, mem, w_in, w_kv, w_out, pre_norm, post_norm, mem_norm, s5_lam_re, s5_lam_im, s5_log_step, s5_b_re, s5_b_im, s5_c_re, s5_c_im, s5_d, s5_w_glu, s5_b_glu, lru_conv_w, lru_conv_b, lru_w_a, lru_b_a, lru_w_x, lru_b_x, lru_lam):
    batch, seq, d = x.shape
    depth = w_in.shape[0]
    xa = w_kv.shape[2] // 2
    rec = w_out.shape[1] - xa
    groups, _, gch = s5_b_re.shape[1:]
    assert seq % (TL * CL) == 0 and TL * gch == LANE_SLAB and rec == groups * gch
    ch = seq // (TL * CL)
    assert ch & (ch - 1) == 0, "row-shift masks assume a power-of-two super-chunk count"

    assert depth >= 2, "first and last layers use distinct residual kernels"

    w_in_cur = w_in[0].astype(BF16)
    tables = jax.vmap(functools.partial(_s5_tables, ch=ch))(
        s5_lam_re, s5_lam_im, s5_log_step, s5_b_re, s5_b_im, s5_c_re, s5_c_im, s5_d)
    stacked = [w_kv, s5_w_glu, lru_w_a, lru_w_x]
    flat = lambda w: w.reshape(1, -1, w.shape[-1])

    hn = _entry(x, pre_norm[0])
    h = None
    for i in range(depth):
        j = i // 2
        casts = [(w_out, i)] + ([(w_in, i + 1)] if i + 1 < depth else [])
        if i == 0:
            casts += [(flat(w), 0) for w in stacked]
        proj, cast_out = _in_proj(hn, w_in_cur, casts)
        w_out_cur = cast_out[0]
        if i + 1 < depth:
            w_in_cur = cast_out[1]
        if i == 0:
            w_kv_b, w_glu_b, w_a_b, w_x_b = [c.reshape(w.shape) for c, w in zip(cast_out[2:], stacked)]
            kv = _kv_proj(mem, mem_norm, w_kv_b)
        if i % 2 == 0:
            gact = _s5_ssm(proj, tables, j, batch, rec)
            ymix = _glu_gate(gact, w_glu_b, s5_b_glu[j].astype(F32), j, proj)
        else:
            ymix = _rglru(proj, lru_conv_w[j], lru_conv_b[j], w_a_b, lru_b_a[j],
                          w_x_b, lru_b_x[j], lru_lam[j], j, batch, rec)
        mmix = _mem_attention(proj, kv, i, batch, rec, xa)
        o = _out_proj(ymix, mmix, w_out_cur)
        if i == 0:
            h, hn = _first_resid_norm(o, x, post_norm[i], pre_norm[i + 1])
        elif i + 1 < depth:
            h, hn = _resid_norm(o, h, post_norm[i], pre_norm[i + 1])
        else:
            h = _final_resid(o, h, post_norm[i], batch, seq)
    return h
```

```python
import functools
import math

import jax
import jax.numpy as jnp
from jax import lax
from jax.experimental import pallas as pl
from jax.experimental.pallas import tpu as pltpu

F32 = jnp.float32
BF16 = jnp.bfloat16

EPS = 1e-6
XA_HEADS = 4
LRU_C = 8.0
TL = 16
CL = 16
LANE_SLAB = 256
VMEM_LIMIT_V7X = 56 * 1024 * 1024


def _tile(n, pref):
    if n <= pref:
        return n
    t = (pref // 128) * 128
    while n % t:
        t -= 128
    return t


def _params(semantics):
    return pltpu.CompilerParams(dimension_semantics=semantics, vmem_limit_bytes=VMEM_LIMIT_V7X)


def _silu(x):
    return x * jax.nn.sigmoid(x)


def _gelu_tanh(x):
    c = math.sqrt(2.0 / math.pi)
    return 0.5 * x * (1.0 + jnp.tanh(c * (x + 0.044715 * (x * x * x))))


def _rms_scale(x):
    return lax.rsqrt(jnp.mean(x * x, axis=-1, keepdims=True) + EPS)


def _time_spec(ch, d):
    return pl.BlockSpec((None, ch, None, TL, d), lambda i, j: (i, 0, j, 0, 0))


def _trunk_spec(ch, d):
    return pl.BlockSpec((None, TL, None, ch, d), lambda i, j: (i, 0, j, 0, 0))


def _entry_kernel(x_ref, g_ref, hn_ref, t_ref):
    g = g_ref[...]
    for ch in range(x_ref.shape[0]):
        t_ref[:, ch, :] = x_ref[ch]
    for tl in range(TL):
        x = t_ref[tl]
        hn_ref[tl] = (x * _rms_scale(x) * g).astype(hn_ref.dtype)


def _entry(x, g):
    b, seq, d = x.shape
    ch = seq // (TL * CL)
    hn = pl.pallas_call(
        _entry_kernel,
        out_shape=jax.ShapeDtypeStruct((b, TL, CL, ch, d), BF16),
        grid=(b, CL),
        in_specs=[_time_spec(ch, d), pl.BlockSpec((1, d), lambda i, j: (0, 0))],
        out_specs=_trunk_spec(ch, d),
        scratch_shapes=[pltpu.VMEM((TL, ch, d), F32)],
        compiler_params=_params(("parallel", "parallel")),
        name="entry_norm",
    )(x.reshape(b, ch, CL, TL, d), g.reshape(1, d))
    return hn.reshape(b * seq, d)


def _first_resid_kernel(o_ref, x_ref, pg_ref, ng_ref, ho_ref, hn_ref):
    pg, ng = pg_ref[...], ng_ref[...]
    for ch in range(x_ref.shape[0]):
        ho_ref[:, ch, :] = x_ref[ch]
    for tl in range(TL):
        o = o_ref[tl].astype(F32)
        h = ho_ref[tl] + o * _rms_scale(o) * pg
        ho_ref[tl] = h
        hn_ref[tl] = (h * _rms_scale(h) * ng).astype(hn_ref.dtype)


def _first_resid_norm(o, x, post_g, next_g):
    b, seq, d = x.shape
    ch = seq // (TL * CL)
    vec = pl.BlockSpec((1, d), lambda i, j: (0, 0))
    h, hn = pl.pallas_call(
        _first_resid_kernel,
        out_shape=(jax.ShapeDtypeStruct((b, TL, CL, ch, d), F32),
                   jax.ShapeDtypeStruct((b, TL, CL, ch, d), BF16)),
        grid=(b, CL),
        in_specs=[_trunk_spec(ch, d), _time_spec(ch, d), vec, vec],
        out_specs=(_trunk_spec(ch, d), _trunk_spec(ch, d)),
        compiler_params=_params(("parallel", "parallel")),
        name="first_resid",
    )(o.reshape(b, TL, CL, ch, d), x.reshape(b, ch, CL, TL, d), post_g.reshape(1, d), next_g.reshape(1, d))
    return h.reshape(b * seq, d), hn.reshape(b * seq, d)


def _in_proj_kernel(x_ref, w_ref, *refs):
    n_cast = (len(refs) - 1) // 2
    o_ref = refs[n_cast]
    o_ref[...] = jnp.dot(x_ref[...], w_ref[...], preferred_element_type=F32).astype(o_ref.dtype)
    for src, dst in zip(refs[:n_cast], refs[n_cast + 1:]):
        dst[...] = src[...].astype(dst.dtype)


def _in_proj(hn, w, casts):
    t, d = hn.shape
    n = w.shape[1]
    tm, tn = _tile(t, 1024), _tile(n, 1024)
    ni, nj = t // tm, n // tn
    steps = ni * nj
    in_specs = [pl.BlockSpec((tm, d), lambda i, j: (i, 0)),
                pl.BlockSpec((d, tn), lambda i, j: (0, j))]
    out_shape = [jax.ShapeDtypeStruct((t, n), BF16)]
    out_specs = [pl.BlockSpec((tm, tn), lambda i, j: (i, j))]
    operands = [hn, w]
    for src, layer in casts:
        _, r, c = src.shape
        strip = r // steps
        assert strip * steps == r and strip % 16 == 0, "bf16 strips must cover whole (16, 128) tiles"
        in_specs.append(pl.BlockSpec((None, strip, c), lambda i, j, layer=layer: (layer, i * nj + j, 0)))
        out_shape.append(jax.ShapeDtypeStruct((r, c), BF16))
        out_specs.append(pl.BlockSpec((strip, c), lambda i, j: (i * nj + j, 0)))
        operands.append(src)
    res = pl.pallas_call(
        _in_proj_kernel,
        out_shape=tuple(out_shape),
        grid=(ni, nj),
        in_specs=in_specs,
        out_specs=tuple(out_specs),
        compiler_params=_params(("parallel", "arbitrary")),
        name="in_proj",
    )(*operands)
    return res[0], list(res[1:])


def _out_matmul_kernel(y_ref, m_ref, w_ref, o_ref, *, rec):
    acc = jnp.dot(y_ref[...], w_ref[0:rec, :], preferred_element_type=F32)
    acc = acc + jnp.dot(m_ref[...], w_ref[rec:, :], preferred_element_type=F32)
    o_ref[...] = acc.astype(o_ref.dtype)


def _out_proj(ymix, mmix, w):
    t, rec = ymix.shape
    xa = mmix.shape[1]
    d = w.shape[1]
    tm, tn = _tile(t, 1024), _tile(d, 1024)
    return pl.pallas_call(
        functools.partial(_out_matmul_kernel, rec=rec),
        out_shape=jax.ShapeDtypeStruct((t, d), BF16),
        grid=(t // tm, d // tn),
        in_specs=[pl.BlockSpec((tm, rec), lambda i, j: (i, 0)),
                  pl.BlockSpec((tm, xa), lambda i, j: (i, 0)),
                  pl.BlockSpec((rec + xa, tn), lambda i, j: (0, j))],
        out_specs=pl.BlockSpec((tm, tn), lambda i, j: (i, j)),
        compiler_params=_params(("parallel", "arbitrary")),
        name="out_proj",
    )(ymix, mmix, w)


def _resid_kernel(o_ref, h_ref, pg_ref, ng_ref, ho_ref, hn_ref):
    o = o_ref[...].astype(F32)
    h = h_ref[...] + o * _rms_scale(o) * pg_ref[...]
    ho_ref[...] = h
    hn_ref[...] = (h * _rms_scale(h) * ng_ref[...]).astype(hn_ref.dtype)


def _resid_norm(o, h, post_g, next_g):
    t, d = h.shape
    tm = _tile(t, 256)
    row = pl.BlockSpec((tm, d), lambda i: (i, 0))
    vec = pl.BlockSpec((1, d), lambda i: (0, 0))
    return pl.pallas_call(
        _resid_kernel,
        out_shape=(jax.ShapeDtypeStruct((t, d), F32), jax.ShapeDtypeStruct((t, d), BF16)),
        grid=(t // tm,),
        in_specs=[row, row, vec, vec],
        out_specs=(row, row),
        compiler_params=_params(("parallel",)),
        name="resid_norm",
    )(o, h, post_g.reshape(1, d), next_g.reshape(1, d))


def _final_kernel(o_ref, h_ref, pg_ref, out_ref):
    pg = pg_ref[...]
    for tl in range(TL):
        o = o_ref[tl].astype(F32)
        out_ref[:, tl, :] = h_ref[tl] + o * _rms_scale(o) * pg


def _final_resid(o, h, post_g, batch, seq):
    d = h.shape[1]
    ch = seq // (TL * CL)
    in_spec = pl.BlockSpec((None, TL, None, ch, d), lambda i, j: (i, 0, j, 0, 0))
    out = pl.pallas_call(
        _final_kernel,
        out_shape=jax.ShapeDtypeStruct((batch, ch, CL, TL, d), F32),
        grid=(batch, CL),
        in_specs=[in_spec, in_spec, pl.BlockSpec((1, d), lambda i, j: (0, 0))],
        out_specs=pl.BlockSpec((None, ch, None, TL, d), lambda i, j: (i, 0, j, 0, 0)),
        compiler_params=_params(("parallel", "parallel")),
        name="final_resid",
    )(o.reshape(batch, TL, CL, ch, d), h.reshape(batch, TL, CL, ch, d), post_g.reshape(1, d))
    return out.reshape(batch, seq, d)


def _kv_kernel(mem_ref, g_ref, w_ref, o_ref):
    x = mem_ref[...]
    xn = (x * _rms_scale(x) * g_ref[...]).astype(BF16)
    o_ref[...] = jnp.dot(xn, w_ref[...], preferred_element_type=F32).astype(o_ref.dtype)


def _kv_proj(mem, mem_norm, w_kv):
    b, n, d = mem.shape
    depth, _, n2 = w_kv.shape
    tn = _tile(n2, 512)
    return pl.pallas_call(
        _kv_kernel,
        out_shape=jax.ShapeDtypeStruct((depth, b, n, n2), BF16),
        grid=(depth, n2 // tn, b),
        in_specs=[pl.BlockSpec((None, n, d), lambda l, j, i: (i, 0, 0)),
                  pl.BlockSpec((None, 1, d), lambda l, j, i: (l, 0, 0)),
                  pl.BlockSpec((None, d, tn), lambda l, j, i: (l, 0, j))],
        out_specs=pl.BlockSpec((None, None, n, tn), lambda l, j, i: (l, i, 0, j)),
        compiler_params=_params(("parallel", "parallel", "arbitrary")),
        name="kv_proj",
    )(mem, mem_norm.reshape(depth, 1, d), w_kv)


def _attn_kernel(q_ref, qg_ref, kv_ref, o_ref, *, xa, hd):
    scale = hd ** -0.5
    for h in range(XA_HEADS):
        q = q_ref[:, h * hd:(h + 1) * hd]
        k = kv_ref[:, h * hd:(h + 1) * hd]
        v = kv_ref[:, xa + h * hd:xa + (h + 1) * hd]
        s = lax.dot_general(q, k, (((1,), (1,)), ((), ())), preferred_element_type=F32) * scale
        p = jnp.exp(s - jnp.max(s, axis=-1, keepdims=True))
        denom = jnp.sum(p, axis=-1, keepdims=True)
        o = jnp.dot(p.astype(BF16), v, preferred_element_type=F32) / denom
        qg = qg_ref[:, h * hd:(h + 1) * hd].astype(F32)
        o_ref[:, h * hd:(h + 1) * hd] = (o * _silu(qg)).astype(o_ref.dtype)


def _mem_attention(proj, kv_l, layer, batch, rec, xa):
    t = proj.shape[0]
    rows = t // batch
    n_mem = kv_l.shape[2]
    tm = _tile(rows, 1024)
    per_b = rows // tm
    qcol = (2 * rec) // xa
    hd = xa // XA_HEADS
    return pl.pallas_call(
        functools.partial(_attn_kernel, xa=xa, hd=hd),
        out_shape=jax.ShapeDtypeStruct((t, xa), BF16),
        grid=(batch, per_b),
        in_specs=[pl.BlockSpec((tm, xa), lambda b, i: (b * per_b + i, qcol)),
                  pl.BlockSpec((tm, xa), lambda b, i: (b * per_b + i, qcol + 1)),
                  pl.BlockSpec((None, None, n_mem, 2 * xa), lambda b, i: (layer, b, 0, 0))],
        out_specs=pl.BlockSpec((tm, xa), lambda b, i: (b * per_b + i, 0)),
        compiler_params=_params(("parallel", "arbitrary")),
        name="mem_attn",
    )(proj, proj, kv_l)


def _cmul(s, a1, a2, half):
    return s * a1 + pltpu.roll(s, half, axis=1) * a2


def _shift_rows(x, k, period, fill):
    rows = lax.broadcasted_iota(jnp.int32, x.shape, 0)
    return jnp.where((rows & (period - 1)) < k, fill, pltpu.roll(x, k, axis=0))


def _s5_kernel(u_ref, kw_ref, be_ref, cp_ref, lt_ref, d_ref, o_ref, m_ref, t1_ref, ug_ref, z_ref, l_ref, s_ref,
               t2_ref, *, c, ch, gs, gch, p):
    p2 = 2 * p

    @pl.when(pl.program_id(1) == 0)
    def _():
        lanes = lax.broadcasted_iota(jnp.int32, (gch, TL * gch), 1)
        for g in range(gs):
            kw = kw_ref[g]
            m_ref[g, 0:gch, :] = kw.astype(m_ref.dtype)
            for tin in range(1, TL):
                blk = jnp.where(lanes < tin * gch, 0.0, pltpu.roll(kw, tin * gch, axis=1))
                m_ref[g, tin * gch:(tin + 1) * gch, :] = blk.astype(m_ref.dtype)

    for tl in range(TL):
        t1_ref[tl] = u_ref[tl * c:(tl + 1) * c, :].astype(F32).T
    for g in range(gs):
        vt = t1_ref[:, g * gch:(g + 1) * gch, :].reshape(TL * gch, c)
        ug = vt.T.astype(BF16)
        ug_ref[g] = ug
        z_ref[g] = jnp.dot(ug, be_ref[g], preferred_element_type=F32)

    def ratio(row):
        a1 = jnp.broadcast_to(lt_ref[:, row:row + 1, :], (gs, ch, p2)).reshape(gs * ch, p2)
        a2 = jnp.broadcast_to(lt_ref[:, row + 1:row + 2, :], (gs, ch, p2)).reshape(gs * ch, p2)
        return a1, a2

    def zslab(ref, cl):
        return ref[:, cl * ch:(cl + 1) * ch, :].reshape(gs * ch, p2)

    a1, a2 = ratio(0)
    lc = zslab(z_ref, 0)
    l_ref[:, 0:ch, :] = lc.reshape(gs, ch, p2)
    for cl in range(1, CL):
        lc = _cmul(lc, a1, a2, p) + zslab(z_ref, cl)
        l_ref[:, cl * ch:(cl + 1) * ch, :] = lc.reshape(gs, ch, p2)
    tot = lc
    k, row = 1, 2 + 2 * CL
    while k < ch:
        b1, b2 = ratio(row)
        tot = tot + _cmul(_shift_rows(tot, k, ch, 0.0), b1, b2, p)
        k, row = 2 * k, row + 2
    cy = _shift_rows(tot, 1, ch, 0.0)
    s_ref[:, 0:ch, :] = cy.reshape(gs, ch, p2).astype(s_ref.dtype)
    for cl in range(1, CL):
        b1, b2 = ratio(2 + 2 * cl)
        s_in = zslab(l_ref, cl - 1) + _cmul(cy, b1, b2, p)
        s_ref[:, cl * ch:(cl + 1) * ch, :] = s_in.reshape(gs, ch, p2).astype(s_ref.dtype)

    for g in range(gs):
        ug = ug_ref[g]
        y = jnp.dot(ug, m_ref[g], preferred_element_type=F32)
        y = y + jnp.dot(s_ref[g], cp_ref[g], preferred_element_type=F32)
        y = y + d_ref[g] * ug.astype(F32)
        t2_ref[:, g * gch:(g + 1) * gch, :] = _gelu_tanh(y).T.reshape(TL, gch, c)
    for tl in range(TL):
        o_ref[tl * c:(tl + 1) * c, :] = t2_ref[tl].T.astype(o_ref.dtype)


def _s5_tables(lam_re, lam_im, log_step, b_re, b_im, c_re, c_im, d_skip, ch):
    g, p, c = b_re.shape
    hi = lax.Precision.HIGHEST
    lr = jnp.minimum(lam_re.astype(F32), -1e-4)
    li = lam_im.astype(F32)
    dt = jnp.exp(log_step.astype(F32))[:, None]

    def lam_pow(k):
        k = jnp.asarray(k, F32)[..., None, None]
        mag = jnp.exp(lr * dt * k)
        return mag * jnp.cos(li * dt * k), mag * jnp.sin(li * dt * k)

    ab_re, ab_im = lam_pow(1.0)
    den = lr * lr + li * li
    nr, ni = ab_re - 1.0, ab_im
    coef_re = (nr * lr + ni * li) / den
    coef_im = (ni * lr - nr * li) / den
    br, bi = b_re.astype(F32), b_im.astype(F32)
    bb_re = coef_re[..., None] * br - coef_im[..., None] * bi
    bb_im = coef_re[..., None] * bi + coef_im[..., None] * br
    cr, ci = c_re.astype(F32), c_im.astype(F32)

    steps = jnp.arange(TL + 1)
    pr, pi = lam_pow(steps)
    cl_re = cr[None] * pr[:, :, None, :] - ci[None] * pi[:, :, None, :]
    cl_im = cr[None] * pi[:, :, None, :] + ci[None] * pr[:, :, None, :]
    klag = (jnp.einsum('kgop,gpi->gkoi', cl_re[:TL], bb_re, precision=hi)
            - jnp.einsum('kgop,gpi->gkoi', cl_im[:TL], bb_im, precision=hi))
    kw = lax.optimization_barrier(klag).transpose(0, 3, 1, 2).reshape(g, c, TL * c)

    qr, qi = pr[TL - 1 - jnp.arange(TL)], pi[TL - 1 - jnp.arange(TL)]
    be_re = qr[..., None] * bb_re[None] - qi[..., None] * bb_im[None]
    be_im = qr[..., None] * bb_im[None] + qi[..., None] * bb_re[None]
    bend = jnp.concatenate([be_re, be_im], axis=2)
    bend = bend.transpose(1, 0, 3, 2).reshape(g, TL * c, 2 * p)

    cp = jnp.concatenate([cl_re[1:], -cl_im[1:]], axis=3)
    cpow = cp.transpose(1, 3, 0, 2).reshape(g, 2 * p, TL * c)

    ks = [float(TL)] + [float(TL * cl) for cl in range(CL)]
    k = 1
    while k < ch:
        ks.append(float(TL * CL * k))
        k *= 2
    wr, wi = lam_pow(jnp.array(ks, F32))
    rows = jnp.stack([jnp.concatenate([wr, wr], axis=-1),
                      jnp.concatenate([-wi, wi], axis=-1)], axis=1)
    ltab = rows.reshape(2 * len(ks), g, 2 * p).transpose(1, 0, 2)
    pad = (-ltab.shape[1]) % 8
    ltab = jnp.pad(ltab, ((0, 0), (0, pad), (0, 0)))
    dtile = jnp.tile(d_skip.astype(F32).reshape(g, 1, c), (1, 1, TL))
    return kw, bend.astype(BF16), cpow.astype(BF16), ltab, dtile


def _s5_ssm(proj, tables, layer, batch, rec):
    kw, bend, cpow, ltab, dtile = tables
    t = proj.shape[0]
    rows = t // batch
    c = rows // TL
    _, g, w, p2 = bend.shape
    gch = w // TL
    gs = LANE_SLAB // gch
    nt = ltab.shape[2]
    wspec = lambda *shape: pl.BlockSpec((None, gs) + shape, lambda j, b: (layer, j, 0, 0))
    return pl.pallas_call(
        functools.partial(_s5_kernel, c=c, ch=c // CL, gs=gs, gch=gch, p=p2 // 2),
        out_shape=jax.ShapeDtypeStruct((t, rec), BF16),
        grid=(g // gs, batch),
        in_specs=[pl.BlockSpec((rows, LANE_SLAB), lambda j, b: (b, j)),
                  wspec(gch, w), wspec(w, p2), wspec(p2, w), wspec(nt, p2), wspec(1, w)],
        out_specs=pl.BlockSpec((rows, LANE_SLAB), lambda j, b: (b, j)),
        scratch_shapes=[pltpu.VMEM((gs, w, w), BF16),
                        pltpu.VMEM((TL, LANE_SLAB, c), F32),
                        pltpu.VMEM((gs, c, w), BF16),
                        pltpu.VMEM((gs, c, p2), F32),
                        pltpu.VMEM((gs, c, p2), F32),
                        pltpu.VMEM((gs, c, p2), BF16),
                        pltpu.VMEM((TL, LANE_SLAB, c), F32)],
        compiler_params=_params(("parallel", "arbitrary")),
        name="s5_ssm",
    )(proj, kw, bend, cpow, ltab, dtile)


def _glu_kernel(g_ref, w_ref, b_ref, gn_ref, gate_ref, o_ref):
    z = jnp.dot(g_ref[...], w_ref[...], preferred_element_type=F32) + b_ref[...]
    gn = gn_ref[...].astype(F32)
    gate = gate_ref[...].astype(F32)
    o_ref[...] = (gn * jax.nn.sigmoid(z) * _silu(gate)).astype(o_ref.dtype)


def _glu_gate(gact, w_all, b_glu, layer, proj):
    t, rec = gact.shape
    tm, tn = _tile(t, 1024), _tile(rec, 1024)
    goff = rec // tn
    return pl.pallas_call(
        _glu_kernel,
        out_shape=jax.ShapeDtypeStruct((t, rec), BF16),
        grid=(t // tm, rec // tn),
        in_specs=[pl.BlockSpec((tm, rec), lambda i, j: (i, 0)),
                  pl.BlockSpec((None, rec, tn), lambda i, j: (layer, 0, j)),
                  pl.BlockSpec((1, tn), lambda i, j: (0, j)),
                  pl.BlockSpec((tm, tn), lambda i, j: (i, j)),
                  pl.BlockSpec((tm, tn), lambda i, j: (i, goff + j))],
        out_specs=pl.BlockSpec((tm, tn), lambda i, j: (i, j)),
        compiler_params=_params(("parallel", "arbitrary")),
        name="s5_glu",
    )(gact, w_all, b_glu.reshape(1, rec), gact, proj)


def _prev_chunk(x, ch):
    rows = lax.broadcasted_iota(jnp.int32, x.shape, 0)
    wrapped = jnp.where(rows == 0, 0.0, pltpu.roll(x, ch + 1, axis=0))
    return jnp.where(rows < ch, wrapped, pltpu.roll(x, ch, axis=0))


def _lru_kernel(u_ref, gate_ref, cw_ref, cb_ref, wa_ref, ba_ref, wx_ref, bx_ref, lam_ref, o_ref,
                hs_ref, ps_ref, uf_ref, *, c, ch):
    neg_lam = -lam_ref[...]
    sp = jnp.maximum(neg_lam, 0.0) + jnp.log1p(jnp.exp(-jnp.abs(neg_lam)))
    cw = cw_ref[...]
    cb = cb_ref[...]
    wa, wx = wa_ref[...], wx_ref[...]
    ba, bx = ba_ref[...], bx_ref[...]
    conv_w = cw.shape[0]
    uf_ref[...] = u_ref[...].astype(F32)

    def slab(tl):
        return uf_ref[tl * c:(tl + 1) * c, :]

    h = None
    pr = None
    for tl in range(TL):
        xc = cb + cw[conv_w - 1:conv_w] * slab(tl)
        for j in range(1, conv_w):
            wj = cw[conv_w - 1 - j:conv_w - j]
            if tl - j >= 0:
                xc = xc + wj * slab(tl - j)
            else:
                xc = xc + wj * _prev_chunk(slab(tl - j + TL), ch)
        xcb = xc.astype(BF16)
        r = jax.nn.sigmoid(jnp.dot(xcb, wa, preferred_element_type=F32) + ba)
        ig = jax.nn.sigmoid(jnp.dot(xcb, wx, preferred_element_type=F32) + bx)
        log_a = (-LRU_C) * r * sp
        a = jnp.exp(log_a)
        mult = jnp.sqrt(1.0 - a * a)
        if tl == 0:
            rows = lax.broadcasted_iota(jnp.int32, mult.shape, 0)
            mult = jnp.where(rows == 0, 1.0, mult)
        bt = mult * (ig * xc)
        if tl == 0:
            h, pr = bt, a
        else:
            h, pr = a * h + bt, a * pr
        hs_ref[tl * c:(tl + 1) * c, :] = h
        ps_ref[tl * c:(tl + 1) * c, :] = pr
    av, bv = pr[0:ch], h[0:ch]
    a_pre, b_pre = [av], [bv]
    for cl in range(1, CL):
        pc, hc = pr[cl * ch:(cl + 1) * ch], h[cl * ch:(cl + 1) * ch]
        av, bv = pc * av, pc * bv + hc
        a_pre.append(av)
        b_pre.append(bv)
    k = 1
    while k < ch:
        bv = bv + av * _shift_rows(bv, k, ch, 0.0)
        av = av * _shift_rows(av, k, ch, 1.0)
        k *= 2
    cy = _shift_rows(bv, 1, ch, 0.0)
    carry = jnp.concatenate([cy] + [a_pre[cl - 1] * cy + b_pre[cl - 1] for cl in range(1, CL)], axis=0)
    for tl in range(TL):
        hf = hs_ref[tl * c:(tl + 1) * c, :] + ps_ref[tl * c:(tl + 1) * c, :] * carry
        gate = gate_ref[tl * c:(tl + 1) * c, :].astype(F32)
        o_ref[tl * c:(tl + 1) * c, :] = (hf * _silu(gate)).astype(o_ref.dtype)


def _rglru(proj, conv_w, conv_b, w_a_all, b_a, w_x_all, b_x, lam, layer, batch, rec):
    t = proj.shape[0]
    rows = t // batch
    _, nblk, blk, _ = w_a_all.shape
    cwid = conv_w.shape[0]
    goff = rec // blk
    c = rows // TL
    vec = lambda v: v.astype(F32).reshape(nblk, 1, blk)
    wspec = pl.BlockSpec((None, None, blk, blk), lambda b, j: (layer, j, 0, 0))
    return pl.pallas_call(
        functools.partial(_lru_kernel, c=c, ch=c // CL),
        out_shape=jax.ShapeDtypeStruct((t, rec), BF16),
        grid=(batch, nblk),
        in_specs=[pl.BlockSpec((rows, blk), lambda b, j: (b, j)),
                  pl.BlockSpec((rows, blk), lambda b, j: (b, goff + j)),
                  pl.BlockSpec((cwid, blk), lambda b, j: (0, j)),
                  pl.BlockSpec((None, 1, blk), lambda b, j: (j, 0, 0)),
                  wspec,
                  pl.BlockSpec((None, 1, blk), lambda b, j: (j, 0, 0)),
                  wspec,
                  pl.BlockSpec((None, 1, blk), lambda b, j: (j, 0, 0)),
                  pl.BlockSpec((None, 1, blk), lambda b, j: (j, 0, 0))],
        out_specs=pl.BlockSpec((rows, blk), lambda b, j: (b, j)),
        scratch_shapes=[pltpu.VMEM((rows, blk), F32)] * 3,
        compiler_params=_params(("parallel", "arbitrary")),
        name="rglru",
    )(proj, proj, conv_w.astype(F32), vec(conv_b), w_a_all, vec(b_a), w_x_all, vec(b_x), vec(lam))


def kernel(x, mem, w_in, w_kv, w_out, pre_norm, post_norm, mem_norm, s5_lam_re, s5_lam_im, s5_log_step, s5_b_re, s5_b_im, s5_c_re, s5_c_im, s5_d, s5_w_glu, s5_b_glu, lru_conv_w, lru_conv_b, lru_w_a, lru_b_a, lru_w_x, lru_b_x, lru_lam):
    batch, seq, d = x.shape
    depth = w_in.shape[0]
    xa = w_kv.shape[2] // 2
    rec = w_out.shape[1] - xa
    groups, _, gch = s5_b_re.shape[1:]
    assert seq % (TL * CL) == 0 and TL * gch == LANE_SLAB and rec == groups * gch
    ch = seq // (TL * CL)
    assert ch & (ch - 1) == 0, "row-shift masks assume a power-of-two super-chunk count"

    assert depth >= 2, "first and last layers use distinct residual kernels"

    w_in_cur = w_in[0].astype(BF16)
    tables = jax.vmap(functools.partial(_s5_tables, ch=ch))(
        s5_lam_re, s5_lam_im, s5_log_step, s5_b_re, s5_b_im, s5_c_re, s5_c_im, s5_d)
    stacked = [w_kv, s5_w_glu, lru_w_a, lru_w_x]
    flat = lambda w: w.reshape(1, -1, w.shape[-1])

    hn = _entry(x, pre_norm[0])
    h = None
    for i in range(depth):
        j = i // 2
        casts = [(w_out, i)] + ([(w_in, i + 1)] if i + 1 < depth else [])
        if i == 0:
            casts += [(flat(w), 0) for w in stacked]
        proj, cast_out = _in_proj(hn, w_in_cur, casts)
        w_out_cur = cast_out[0]
        if i + 1 < depth:
            w_in_cur = cast_out[1]
        if i == 0:
            w_kv_b, w_glu_b, w_a_b, w_x_b = [c.reshape(w.shape) for c, w in zip(cast_out[2:], stacked)]
            kv = _kv_proj(mem, mem_norm, w_kv_b)
        if i % 2 == 0:
            gact = _s5_ssm(proj, tables, j, batch, rec)
            ymix = _glu_gate(gact, w_glu_b, s5_b_glu[j].astype(F32), j, proj)
        else:
            ymix = _rglru(proj, lru_conv_w[j], lru_conv_b[j], w_a_b, lru_b_a[j],
                          w_x_b, lru_b_x[j], lru_lam[j], j, batch, rec)
        mmix = _mem_attention(proj, kv, i, batch, rec, xa)
        o = _out_proj(ymix, mmix, w_out_cur)
        if i == 0:
            h, hn = _first_resid_norm(o, x, post_norm[i], pre_norm[i + 1])
        elif i + 1 < depth:
            h, hn = _resid_norm(o, h, post_norm[i], pre_norm[i + 1])
        else:
            h = _final_resid(o, h, post_norm[i], batch, seq)
    return h
```
